```python
import jax, jax.numpy as jnp
from jax import lax
import numpy as np

D_MODEL = 2048
BATCH = 16
SEQ = 256
DEPTH = 2
DEC_BATCH = 2
DEC_SEQ = 1024
PAST_LEN = 512

GRID_W = 64
N_MIXERS = 2
N_FOURIER_LAYERS = (DEPTH + 1) // 2
N_GLA_LAYERS = DEPTH // 2
N_MOD = 9
D_FF = 5632
FOURIER_GROUPS = 4
FOURIER_GROUP_W = D_MODEL // FOURIER_GROUPS
GLA_HEADS = 4
DK_TOT = D_MODEL // 2
DV_TOT = D_MODEL
HEAD_K = DK_TOT // GLA_HEADS
HEAD_V = DV_TOT // GLA_HEADS
GATE_RANK = 16
GATE_TAU = 16.0
CHUNK = 64
ROPE_PAIRS = HEAD_K // 4
ROPE_BASE = 10000.0
EPS = 1e-6
GLA_IN_COLS = 4 * DK_TOT + 2 * DV_TOT + 2 * GATE_RANK
GLA_SPLITS = [DK_TOT, 2 * DK_TOT, 3 * DK_TOT, 4 * DK_TOT, 4 * DK_TOT + DV_TOT,
              4 * DK_TOT + 2 * DV_TOT, 4 * DK_TOT + 2 * DV_TOT + GATE_RANK]

kernel_name = 'fourier_gla_macaron_diffusion_step'


def rms_norm(x, g):
    xf = x.astype(jnp.float32)
    y = xf * lax.rsqrt(jnp.mean(xf * xf, axis=-1, keepdims=True) + EPS)
    return (y * g.astype(jnp.float32)).astype(x.dtype)


def modulation(cond, w, b):
    m = jax.nn.silu(cond) @ w + b
    return m.reshape(cond.shape[0], N_MOD, D_MODEL)


def pre_mod(x, g, m, k):
    return rms_norm(x, g) * (1 + m[:, 3 * k + 1, None, :]) + m[:, 3 * k, None, :]


def post_add(x, out, g, m, k, w):
    return x + w * m[:, 3 * k + 2, None, :] * rms_norm(out, g)


def swiglu(h, w_gu, w_down):
    gate, up = jnp.split(h @ w_gu, 2, axis=-1)
    return (jax.nn.silu(gate) * up) @ w_down


def fourier_mix(h, w):
    B, T, _ = h.shape
    hg = h.astype(jnp.float32).reshape(B, T, FOURIER_GROUPS, FOURIER_GROUP_W)
    f = jnp.fft.fft2(hg, axes=(1, 3), norm='ortho').real
    return f.reshape(B, T, D_MODEL).astype(h.dtype) @ w


def axial_rope(rows):
    row = jnp.repeat(jnp.arange(rows), GRID_W).astype(jnp.float32)
    col = jnp.tile(jnp.arange(GRID_W), rows).astype(jnp.float32)
    inv = ROPE_BASE ** (-jnp.arange(ROPE_PAIRS, dtype=jnp.float32) / ROPE_PAIRS)
    ang = jnp.concatenate([row[:, None] * inv, col[:, None] * inv], axis=-1)
    return jnp.cos(ang), jnp.sin(ang)


def apply_rope(x, rope):
    cos, sin = rope
    c = cos[None, :, None, :]
    s = sin[None, :, None, :]
    xr = x.astype(jnp.float32).reshape(x.shape[:-1] + (HEAD_K // 2, 2))
    xe, xo = xr[..., 0], xr[..., 1]
    out = jnp.stack([xe * c - xo * s, xe * s + xo * c], axis=-1)
    return out.reshape(x.shape).astype(x.dtype)


def gla_scan(q, k, v, g, s0):
    B, T, H, _ = q.shape
    n = T // CHUNK

    def chunks(a):
        return a.astype(jnp.float32).reshape(B, n, CHUNK, H, a.shape[-1]).transpose(1, 0, 3, 2, 4)

    causal = jnp.tril(jnp.ones((CHUNK, CHUNK), dtype=bool))[:, :, None]

    def step(S, inp):
        qc, kc, vc, gc = inp
        b = jnp.cumsum(gc, axis=2)
        o_inter = jnp.einsum('bhik,bhkv->bhiv', qc * jnp.exp(b), S)
        diff = b[:, :, :, None, :] - b[:, :, None, :, :]
        decay = jnp.exp(jnp.where(causal, diff, -jnp.inf))
        scores = jnp.einsum('bhik,bhjk,bhijk->bhij', qc, kc, decay)
        o = o_inter + jnp.einsum('bhij,bhjv->bhiv', scores, vc)
        b_last = b[:, :, -1:, :]
        S = jnp.exp(b_last[:, :, 0, :, None]) * S + jnp.einsum(
            'bhjk,bhjv->bhkv', kc * jnp.exp(b_last - b), vc)
        return S, o

    S, o = lax.scan(step, s0.astype(jnp.float32), (chunks(q), chunks(k), chunks(v), chunks(g)))
    o = o.transpose(1, 0, 3, 2, 4).reshape(B, T, H, v.shape[-1])
    return o.astype(v.dtype), S.astype(v.dtype)


def gla_mix(h, w_in, wg_up, b_g, g_norm, w_out, s0, rope):
    B, T, _ = h.shape
    qf, kf, qb, kb, v, r, lr_f, lr_b = jnp.split(h @ w_in, GLA_SPLITS, axis=-1)

    def heads(a):
        return a.reshape(B, T, GLA_HEADS, -1)

    qf, kf, qb, kb, v, r = (heads(a) for a in (qf, kf, qb, kb, v, r))
    if rope is not None:
        qf, kf, qb, kb = (apply_rope(a, rope) for a in (qf, kf, qb, kb))

    def log_decay(lr, d):
        return heads(jax.nn.log_sigmoid((lr @ wg_up[d] + b_g[d]).astype(jnp.float32)) / GATE_TAU)

    scale = HEAD_K ** -0.5
    o_f, s_f = gla_scan(qf * scale, kf, v, log_decay(lr_f, 0), s0[:, 0])

    def flip(a):
        return jnp.flip(a, axis=1)

    o_b, s_b = gla_scan(flip(qb) * scale, flip(kb), flip(v), flip(log_decay(lr_b, 1)), s0[:, 1])
    o = rms_norm(o_f + flip(o_b), g_norm) * jax.nn.silu(r)
    return o.reshape(B, T, DV_TOT) @ w_out, jnp.stack([s_f, s_b], axis=1)


def trunk(x, cond, gla_init, rope, ada_w, ada_b, norm_pre, norm_post, ffn_w_gate_up,
          ffn_w_down, fourier_w, gla_w_in, gla_w_gate_up, gla_b_gate, gla_norm, gla_w_out):
    B = x.shape[0]
    states = []
    for l in range(DEPTH):
        m = modulation(cond, ada_w[l], ada_b[l])
        h = pre_mod(x, norm_pre[l, 0], m, 0)
        x = post_add(x, swiglu(h, ffn_w_gate_up[l, 0], ffn_w_down[l, 0]), norm_post[l, 0], m, 0, 0.5)
        h = pre_mod(x, norm_pre[l, 1], m, 1)
        j = l // N_MIXERS
        if l % N_MIXERS == 0:
            out = fourier_mix(h, fourier_w[j])
        else:
            if gla_init is None:
                s0 = jnp.zeros((B, 2, GLA_HEADS, HEAD_K, HEAD_V), x.dtype)
            else:
                s0 = gla_init[:, j]
            out, st = gla_mix(h, gla_w_in[j], gla_w_gate_up[j], gla_b_gate[j], gla_norm[j],
                              gla_w_out[j], s0, rope)
            states.append(st)
        x = post_add(x, out, norm_post[l, 1], m, 1, 1.0)
        h = pre_mod(x, norm_pre[l, 2], m, 2)
        x = post_add(x, swiglu(h, ffn_w_gate_up[l, 1], ffn_w_down[l, 1]), norm_post[l, 2], m, 2, 0.5)
    return x, states


def setup_inputs(seed: int = 0) -> dict:
    key = jax.random.key(seed)
    ks = jax.random.split(key, 17)
    f32 = jnp.float32
    nrm = lambda k, s, sc: jax.random.normal(k, s, f32) * sc
    return {
        'x_prompt': nrm(ks[0], (BATCH, SEQ, D_MODEL), 1.0),
        'x_sample': nrm(ks[1], (DEC_BATCH, DEC_SEQ, D_MODEL), 1.0),
        'state_gla': nrm(ks[2], (DEC_BATCH, N_GLA_LAYERS, 2, GLA_HEADS, HEAD_K, HEAD_V), 0.5),
        'c': nrm(ks[3], (DEC_BATCH, D_MODEL), 1.0),
        'c_ctx': nrm(ks[4], (D_MODEL,), 1.0),
        'ada_w': nrm(ks[5], (DEPTH, D_MODEL, N_MOD * D_MODEL), 0.5 * D_MODEL ** -0.5),
        'ada_b': nrm(ks[6], (DEPTH, N_MOD * D_MODEL), 0.02),
        'norm_pre': 1.0 + nrm(ks[7], (DEPTH, 3, D_MODEL), 0.02),
        'norm_post': 1.0 + nrm(ks[8], (DEPTH, 3, D_MODEL), 0.02),
        'ffn_w_gate_up': nrm(ks[9], (DEPTH, 2, D_MODEL, 2 * D_FF), D_MODEL ** -0.5),
        'ffn_w_down': nrm(ks[10], (DEPTH, 2, D_FF, D_MODEL), D_FF ** -0.5),
        'fourier_w': nrm(ks[11], (N_FOURIER_LAYERS, D_MODEL, D_MODEL), D_MODEL ** -0.5),
        'gla_w_in': nrm(ks[12], (N_GLA_LAYERS, D_MODEL, GLA_IN_COLS), D_MODEL ** -0.5),
        'gla_w_gate_up': nrm(ks[13], (N_GLA_LAYERS, 2, GATE_RANK, DK_TOT), GATE_RANK ** -0.5),
        'gla_b_gate': nrm(ks[14], (N_GLA_LAYERS, 2, DK_TOT), 0.1),
        'gla_norm': 1.0 + nrm(ks[15], (N_GLA_LAYERS, HEAD_V), 0.02),
        'gla_w_out': nrm(ks[16], (N_GLA_LAYERS, DV_TOT, D_MODEL), DV_TOT ** -0.5),
    }


def reference(x_prompt, x_sample, state_gla, c, c_ctx, ada_w, ada_b, norm_pre, norm_post,
              ffn_w_gate_up, ffn_w_down, fourier_w, gla_w_in, gla_w_gate_up, gla_b_gate,
              gla_norm, gla_w_out):
    y_prompt, ctx_states = trunk(x_prompt, c_ctx[None, :], None, None, ada_w, ada_b, norm_pre,
                                 norm_post, ffn_w_gate_up, ffn_w_down, fourier_w, gla_w_in,
                                 gla_w_gate_up, gla_b_gate, gla_norm, gla_w_out)
    new_state_gla = jnp.stack(ctx_states, axis=1)
    rows = x_sample.shape[1] // GRID_W
    rope = axial_rope(rows)
    y_sample, _ = trunk(x_sample, c, state_gla, rope, ada_w, ada_b, norm_pre, norm_post,
                        ffn_w_gate_up, ffn_w_down, fourier_w, gla_w_in, gla_w_gate_up,
                        gla_b_gate, gla_norm, gla_w_out)
    return (y_prompt, y_sample, new_state_gla)
```

```python
import functools

import numpy as np
import jax
import jax.numpy as jnp
from jax import lax
from jax.experimental import pallas as pl
from jax.experimental.pallas import tpu as pltpu

F32 = jnp.float32
BF16 = jnp.bfloat16

EPS = 1e-6
N_MOD = 9
GRID_W = 64
FOURIER_GROUPS = 4
GATE_TAU = 16.0
CHUNK = 64
ROPE_BASE = 10000.0
LOG2_E = 1.4426950408889634
N_LEVELS = 6
assert CHUNK == 1 << N_LEVELS

V7X_VMEM_LIMIT_BYTES = 56 * 1024 * 1024
LANES = 128
SUBLANES = 8


def _params(*sem):
    return pltpu.CompilerParams(dimension_semantics=sem, vmem_limit_bytes=V7X_VMEM_LIMIT_BYTES)


def _dot(a, b):
    return jnp.dot(a, b, preferred_element_type=F32)


def _dot_nt(a, b):
    return lax.dot_general(a, b, (((1,), (1,)), ((), ())), preferred_element_type=F32)


def _dot_tn(a, b):
    return lax.dot_general(a, b, (((0,), (0,)), ((), ())), preferred_element_type=F32)


def _rms(x):
    return x * lax.rsqrt(jnp.mean(x * x, axis=-1, keepdims=True) + EPS)


def _pre_mod(x, gpre, mod_ref, k):
    gs = gpre * (1.0 + mod_ref[3 * k + 1:3 * k + 2, :])
    return _rms(x) * gs + mod_ref[3 * k:3 * k + 1, :]


def _post_add(x, out, gpost, mod_ref, k, res_w):
    gg = (res_w * mod_ref[3 * k + 2:3 * k + 3, :]) * gpost
    return x + _rms(out) * gg


def _mod_index(i, tm, group_rows, n_ctx_groups):
    return jnp.maximum(i * tm // group_rows - (n_ctx_groups - 1), 0)


def _on_tile_ref(refs, i, n_ctx_tiles, fn):
    if len(refs) == 1:
        fn(refs[0])
    else:
        pl.when(i < n_ctx_tiles)(lambda: fn(refs[0]))
        pl.when(i >= n_ctx_tiles)(lambda: fn(refs[1]))


def _split_row_specs(arrays, tm, width, n_ctx_tiles, ngrid, buffers=None):
    kw = {} if buffers is None else dict(pipeline_mode=pl.Buffered(buffers))
    if ngrid == 1:
        wrap = lambda f: (lambda i: f(i))
    else:
        wrap = lambda f: (lambda i, j: f(i))
    if len(arrays) == 1:
        return [pl.BlockSpec((tm, width), wrap(lambda i: (i, 0)), **kw)]
    return [
        pl.BlockSpec((tm, width), wrap(lambda i: (jnp.minimum(i, n_ctx_tiles - 1), 0)), **kw),
        pl.BlockSpec((tm, width), wrap(lambda i: (jnp.maximum(i - n_ctx_tiles, 0), 0)), **kw),
    ]


def _mod_kernel(c_ref, w_ref, b_ref, o_ref):
    c = c_ref[...]
    s = (c * jax.nn.sigmoid(c)).astype(BF16)
    o_ref[...] = _dot(s, w_ref[...].astype(BF16)) + b_ref[...]


def _modulation(cond8, ada_w, ada_b):
    depth, d, n = ada_w.shape
    tn = min(n, 1024)
    return pl.pallas_call(
        _mod_kernel,
        grid=(depth, n // tn),
        in_specs=[
            pl.BlockSpec((SUBLANES, d), lambda l, j: (0, 0)),
            pl.BlockSpec((None, d, tn), lambda l, j: (l, 0, j)),
            pl.BlockSpec((None, 1, tn), lambda l, j: (l, 0, j)),
        ],
        out_specs=pl.BlockSpec((None, SUBLANES, tn), lambda l, j: (l, 0, j)),
        out_shape=jax.ShapeDtypeStruct((depth, SUBLANES, n), F32),
        compiler_params=_params("arbitrary", "arbitrary"),
        name="modulation",
    )(cond8, ada_w, ada_b.reshape(depth, 1, n))


def _ffn_kernel(*refs, n_x, n_ctx_tiles, k, res_w):
    x_refs = refs[:n_x]
    mod_ref, gpre_ref, gpost_ref, wg_ref, wu_ref, wd_ref, o_ref, h_ref = refs[n_x:]
    i = pl.program_id(0)
    j = pl.program_id(1)

    @pl.when(j == 0)
    def _():
        def prologue(x_ref):
            h_ref[...] = _pre_mod(x_ref[...], gpre_ref[...], mod_ref, k).astype(BF16)

        _on_tile_ref(x_refs, i, n_ctx_tiles, prologue)
        o_ref[...] = jnp.zeros_like(o_ref)

    h = h_ref[...]
    g = _dot(h, wg_ref[...])
    u = _dot(h, wu_ref[...])
    a = (g * jax.nn.sigmoid(g) * u).astype(BF16)
    o_ref[...] += _dot(a, wd_ref[...])

    @pl.when(j == pl.num_programs(1) - 1)
    def _():
        def epilogue(x_ref):
            o_ref[...] = _post_add(x_ref[...], o_ref[...], gpost_ref[...], mod_ref, k, res_w)

        _on_tile_ref(x_refs, i, n_ctx_tiles, epilogue)


def _ffn(xs, mod, gpre, gpost, wgu, wd, *, layer, slot, k, tm, tf, x_buffers, group_rows, n_ctx_groups):
    m = sum(x.shape[0] for x in xs)
    d = xs[0].shape[1]
    nf = wd.shape[2] // tf
    n_ctx_tiles = n_ctx_groups * group_rows // tm
    midx = functools.partial(_mod_index, tm=tm, group_rows=group_rows, n_ctx_groups=n_ctx_groups)
    return pl.pallas_call(
        functools.partial(_ffn_kernel, n_x=len(xs), n_ctx_tiles=n_ctx_tiles, k=k, res_w=0.5),
        grid=(m // tm, nf),
        in_specs=_split_row_specs(xs, tm, d, n_ctx_tiles, 2, x_buffers) + [
            pl.BlockSpec((None, N_MOD, d), lambda i, j: (midx(i), 0, 0)),
            pl.BlockSpec((1, d), lambda i, j: (0, 0)),
            pl.BlockSpec((1, d), lambda i, j: (0, 0)),
            pl.BlockSpec((None, None, d, tf), lambda i, j: (layer, slot, 0, j)),
            pl.BlockSpec((None, None, d, tf), lambda i, j: (layer, slot, 0, j + nf)),
            pl.BlockSpec((None, None, tf, d), lambda i, j: (layer, slot, j, 0)),
        ],
        out_specs=pl.BlockSpec((tm, d), lambda i, j: (i, 0)),
        out_shape=jax.ShapeDtypeStruct((m, d), F32),
        scratch_shapes=[pltpu.VMEM((tm, d), BF16)],
        compiler_params=_params("arbitrary", "arbitrary"),
        name="ffn",
    )(*xs, mod, gpre, gpost, wgu, wgu, wd)


def _dense_post_kernel(*refs, n_a, n_ctx_tiles, k, res_w):
    a_refs = refs[:n_a]
    w_ref, x_ref, mod_ref, gpost_ref, o_ref = refs[n_a:]

    def body(a_ref):
        y = _dot(a_ref[...], w_ref[...])
        o_ref[...] = _post_add(x_ref[...], y, gpost_ref[...], mod_ref, k, res_w)

    _on_tile_ref(a_refs, pl.program_id(0), n_ctx_tiles, body)


def _dense_post(a_parts, w, x, mod, gpost, *, widx, k, tm, group_rows, n_ctx_groups):
    m, d = x.shape
    kk = a_parts[0].shape[1]
    n_ctx_tiles = n_ctx_groups * group_rows // tm
    midx = functools.partial(_mod_index, tm=tm, group_rows=group_rows, n_ctx_groups=n_ctx_groups)
    return pl.pallas_call(
        functools.partial(_dense_post_kernel, n_a=len(a_parts), n_ctx_tiles=n_ctx_tiles, k=k, res_w=1.0),
        grid=(m // tm,),
        in_specs=_split_row_specs(a_parts, tm, kk, n_ctx_tiles, 1) + [
            pl.BlockSpec((None, kk, d), lambda i: (widx, 0, 0)),
            pl.BlockSpec((tm, d), lambda i: (i, 0)),
            pl.BlockSpec((None, N_MOD, d), lambda i: (midx(i), 0, 0)),
            pl.BlockSpec((1, d), lambda i: (0, 0)),
        ],
        out_specs=pl.BlockSpec((tm, d), lambda i: (i, 0)),
        out_shape=jax.ShapeDtypeStruct((m, d), F32),
        compiler_params=_params("arbitrary"),
        name="dense_post",
    )(*a_parts, w, x, mod, gpost)


def _dft_tables(t, w):
    def cs(n):
        kn = np.outer(np.arange(n), np.arange(n)) % n
        ang = 2.0 * np.pi * kn.astype(np.float64) / n
        return np.cos(ang) / np.sqrt(n), np.sin(ang) / np.sqrt(n)

    ct, st = cs(t)
    cw, sw = cs(w)
    as_bf16 = lambda a: jnp.asarray(a, dtype=F32).astype(BF16)
    return as_bf16(np.concatenate([ct, st], axis=0)), as_bf16(cw), as_bf16(-sw)


def _fourier_kernel(x_ref, mod_ref, gpre_ref, cst_ref, cw_ref, swn_ref, o_ref, *, k):
    t = x_ref.shape[0]
    w = cw_ref.shape[0]
    h = _pre_mod(x_ref[...], gpre_ref[...], mod_ref, k).astype(BF16)
    cst = cst_ref[...]
    for g in range(FOURIER_GROUPS):
        p = _dot(cst, h[:, g * w:(g + 1) * w]).astype(BF16)
        f = _dot(p[:t], cw_ref[...]) + _dot(p[t:], swn_ref[...])
        o_ref[:, g * w:(g + 1) * w] = f.astype(BF16)


def _fourier(x, mod, gpre, *, t, nb, row_block0, mod0, k):
    d = x.shape[1]
    w = d // FOURIER_GROUPS
    cst, cw, swn = _dft_tables(t, w)
    return pl.pallas_call(
        functools.partial(_fourier_kernel, k=k),
        grid=(nb,),
        in_specs=[
            pl.BlockSpec((t, d), lambda b: (row_block0 + b, 0)),
            pl.BlockSpec((None, N_MOD, d), lambda b: (mod0(b), 0, 0)),
            pl.BlockSpec((1, d), lambda b: (0, 0)),
            pl.BlockSpec((2 * t, t), lambda b: (0, 0)),
            pl.BlockSpec((w, w), lambda b: (0, 0)),
            pl.BlockSpec((w, w), lambda b: (0, 0)),
        ],
        out_specs=pl.BlockSpec((t, d), lambda b: (b, 0)),
        out_shape=jax.ShapeDtypeStruct((nb * t, d), BF16),
        compiler_params=_params("arbitrary"),
        name="fourier",
    )(x, mod, gpre, cst, cw, swn)


def _rope_tables(t, head_k):
    rows = t // GRID_W
    pairs = head_k // 4
    row = np.repeat(np.arange(rows), GRID_W).astype(np.float64)
    col = np.tile(np.arange(GRID_W), rows).astype(np.float64)
    inv = ROPE_BASE ** (-np.arange(pairs, dtype=np.float64) / pairs)
    ang = np.concatenate([row[:, None] * inv, col[:, None] * inv], axis=-1)
    cos, sin = np.cos(ang), np.sin(ang)
    ce = np.repeat(cos, 2, axis=-1)
    se = np.stack([-sin, sin], axis=-1).reshape(t, head_k)
    return jnp.asarray(ce, dtype=F32), jnp.asarray(se, dtype=F32)


def _gla_in_kernel(x_ref, mod_ref, gpre_ref, w_ref, wlr_ref, wz_ref, bz_ref, cs_ref, ce_ref, se_ref,
                   p_ref, g_ref, h_ref, lr_ref, *, k, n_ctx_tiles, n_qk_steps, head_k):
    i = pl.program_id(0)
    j = pl.program_id(1)

    @pl.when(j == 0)
    def _():
        h = _pre_mod(x_ref[...], gpre_ref[...], mod_ref, k).astype(BF16)
        h_ref[...] = h
        lr_ref[...] = _dot(h, wlr_ref[...]).astype(BF16)

    def step(rope):
        z = _dot(lr_ref[...], wz_ref[...]) + bz_ref[...]
        g_ref[...] = (jnp.minimum(z, 0.0) - jnp.log1p(jnp.exp(-jnp.abs(z)))) * (LOG2_E / GATE_TAU)
        h = h_ref[...]
        for s in range(w_ref.shape[1] // head_k):
            cols = slice(s * head_k, (s + 1) * head_k)
            acc = _dot(h, w_ref[:, cols]) * cs_ref[:, cols]
            if rope:
                lane = lax.broadcasted_iota(jnp.int32, acc.shape, 1)
                partner = jnp.where(lane % 2 == 0, pltpu.roll(acc, head_k - 1, 1), pltpu.roll(acc, 1, 1))
                acc = acc * ce_ref[...] + partner * se_ref[...]
            p_ref[:, cols] = acc.astype(BF16)

    rope = jnp.logical_and(i >= n_ctx_tiles, j < n_qk_steps)
    pl.when(rope)(lambda: step(True))
    pl.when(jnp.logical_not(rope))(lambda: step(False))


def _gla_in(x, mod, gpre, w_main, w_lr, wz, bz, colscale, *, k, tm, n_steps, head_k, dk_tot, group_rows,
            n_ctx_groups):
    m, d = x.shape
    n = w_main.shape[1]
    tn = n // n_steps
    gcols = 2 * dk_tot // n_steps
    assert (4 * dk_tot) % tn == 0 and tn % head_k == 0 and gcols % LANES == 0
    n_ctx_tiles = n_ctx_groups * group_rows // tm
    tiles_per_group = group_rows // tm
    ce, se = _rope_tables(group_rows, head_k)
    midx = functools.partial(_mod_index, tm=tm, group_rows=group_rows, n_ctx_groups=n_ctx_groups)

    def tab_idx(i, j):
        return (jnp.where(i >= n_ctx_tiles, (i - n_ctx_tiles) % tiles_per_group, 0), 0)

    return pl.pallas_call(
        functools.partial(_gla_in_kernel, k=k, n_ctx_tiles=n_ctx_tiles, n_qk_steps=4 * dk_tot // tn,
                          head_k=head_k),
        grid=(m // tm, n_steps),
        in_specs=[
            pl.BlockSpec((tm, d), lambda i, j: (i, 0)),
            pl.BlockSpec((None, N_MOD, d), lambda i, j: (midx(i), 0, 0)),
            pl.BlockSpec((1, d), lambda i, j: (0, 0)),
            pl.BlockSpec((d, tn), lambda i, j: (0, j)),
            pl.BlockSpec(w_lr.shape, lambda i, j: (0, 0)),
            pl.BlockSpec((wz.shape[0], gcols), lambda i, j: (0, j)),
            pl.BlockSpec((1, gcols), lambda i, j: (0, j)),
            pl.BlockSpec((1, tn), lambda i, j: (0, j)),
            pl.BlockSpec((tm, head_k), tab_idx),
            pl.BlockSpec((tm, head_k), tab_idx),
        ],
        out_specs=[
            pl.BlockSpec((tm, tn), lambda i, j: (i, j)),
            pl.BlockSpec((tm, gcols), lambda i, j: (i, j)),
        ],
        out_shape=[
            jax.ShapeDtypeStruct((m, n), BF16),
            jax.ShapeDtypeStruct((m, 2 * dk_tot), F32),
        ],
        scratch_shapes=[pltpu.VMEM((tm, d), BF16), pltpu.VMEM((tm, w_lr.shape[1]), BF16)],
        compiler_params=_params("arbitrary", "arbitrary"),
        name="gla_in",
    )(x, mod, gpre, w_main, w_lr, wz, bz, colscale, ce, se)


def _scan_constants():
    c = CHUNK
    idx = np.arange(c)
    tri = (idx[None, :] <= idx[:, None]).astype(np.float64)
    masks = [np.eye(c)]
    for lev in range(N_LEVELS):
        h = 1 << lev
        start = idx // (2 * h) * (2 * h)
        upper = (idx - start) >= h
        same = start[:, None] == start[None, :]
        masks.append((same & upper[:, None] & ~upper[None, :]).astype(np.float64))
    m_f = np.stack(masks, axis=0)
    m_b = m_f[:, ::-1, ::-1]
    t_f = np.concatenate([tri, tri], axis=1)
    t_b = np.concatenate([tri[::-1, ::-1]] * 2, axis=1)
    return (jnp.asarray(t_f, dtype=BF16), jnp.asarray(t_b, dtype=BF16),
            jnp.asarray(m_f, dtype=F32), jnp.asarray(m_b, dtype=F32))


def _level_exponents(b, g, fwd):
    c, kk = b.shape
    tile = (c // SUBLANES, SUBLANES, kk)
    b8 = b.reshape(tile)
    r8 = lax.broadcasted_iota(jnp.int32, tile, 1)
    out = [jnp.where(r8 % 2 == (1 if fwd else 0), g.reshape(tile), 0.0).reshape(c, kk)]
    lo, hi = (1, 5) if fwd else (2, 6)
    mids = [jnp.where(r8 < 4, b8[:, lo:lo + 1, :], b8[:, hi:hi + 1, :]), b8[:, 3:4, :] if fwd else b8[:, 4:5, :]]
    for lev, mid in zip((1, 2), mids):
        h = 1 << lev
        second_half = (r8 % (2 * h)) >= h
        sign = jnp.where(second_half if fwd else jnp.logical_not(second_half), 1.0, -1.0)
        out.append(((b8 - mid) * sign).reshape(c, kk))
    for lev in range(3, N_LEVELS):
        h = 1 << lev
        bb = b.reshape(c // (2 * h), 2 * h, kk)
        first, second = bb[:, :h, :], bb[:, h:, :]
        if fwd:
            mid = bb[:, h - 1:h, :]
            t = jnp.concatenate([mid - first, second - mid], axis=1)
        else:
            mid = bb[:, h:h + 1, :]
            t = jnp.concatenate([first - mid, mid - second], axis=1)
        out.append(t.reshape(c, kk))
    return out


def _scan_chunks(chains):
    c = CHUNK

    st1 = []
    for q_ref, k_ref, g_ref, v_ref, tri_ref, m_ref, s_ref, o_ref, rows, kcols, vcols, fwd in chains:
        g = g_ref[rows, kcols]
        g_hi = g.astype(BF16)
        g_lo = (g - g_hi.astype(F32)).astype(BF16)
        b = _dot(tri_ref[...], jnp.concatenate([g_hi, g_lo], axis=0))
        diag = _dot_nt(q_ref[rows, kcols], k_ref[rows, kcols])
        st1.append((g, b, diag))

    st2 = []
    for (q_ref, k_ref, g_ref, v_ref, tri_ref, m_ref, s_ref, o_ref, rows, kcols, vcols, fwd), (g, b, diag) in zip(
            chains, st1):
        q_bf = q_ref[rows, kcols]
        k_bf = k_ref[rows, kcols]
        scores = m_ref[0] * diag
        for lev, t in enumerate(_level_exponents(b, g, fwd)):
            e = jnp.exp2(t).astype(BF16)
            scores += m_ref[lev + 1] * _dot_nt(q_bf * e, k_bf * e)
        st2.append(scores)

    for (q_ref, k_ref, g_ref, v_ref, tri_ref, m_ref, s_ref, o_ref, rows, kcols, vcols, fwd), (g, b, diag), scores in zip(
            chains, st1, st2):
        qe = q_ref[rows, kcols] * jnp.exp2(b).astype(BF16)
        o_ref[rows, vcols] = _dot(qe, s_ref[...].astype(BF16)) + _dot(scores.astype(BF16), v_ref[rows, vcols])

    for (q_ref, k_ref, g_ref, v_ref, tri_ref, m_ref, s_ref, o_ref, rows, kcols, vcols, fwd), (g, b, diag) in zip(
            chains, st1):
        last = c - 1 if fwd else 0
        b_last = b[last:last + 1]
        kt = k_ref[rows, kcols] * jnp.exp2(b_last - b).astype(BF16)
        e_col = jnp.broadcast_to(jnp.exp2(b_last), (LANES, b.shape[1])).T
        s = s_ref[...]
        decay = jnp.concatenate([e_col] * (s.shape[1] // LANES), axis=1)
        s_ref[...] = decay * s + _dot_tn(kt, v_ref[rows, vcols])


def _gla_scan_kernel(*refs, has_s0, has_sout, head_k, head_v):
    qf_ref, kf_ref, qb_ref, kb_ref, v_ref, r_ref, gf_ref, gb_ref = refs[:8]
    pos = 8
    if has_s0:
        s0_ref = refs[pos]
        pos += 1
    tf_ref, tb_ref, mf_ref, mb_ref, gn_ref = refs[pos:pos + 5]
    pos += 5
    o_ref = refs[pos]
    pos += 1
    if has_sout:
        sout_ref = refs[pos]
        pos += 1
    s_ref, of_ref, ob_ref = refs[pos:pos + 3]

    t = v_ref.shape[0]
    n = t // CHUNK
    hps = v_ref.shape[1] // head_v
    if has_s0:
        s_ref[...] = s0_ref[...]
    else:
        s_ref[...] = jnp.zeros_like(s_ref)

    def body(ci, carry):
        rows_f = pl.ds(pl.multiple_of(ci * CHUNK, CHUNK), CHUNK)
        rows_b = pl.ds(pl.multiple_of((n - 1 - ci) * CHUNK, CHUNK), CHUNK)
        chains = []
        for hh in range(hps):
            kcols = slice(hh * head_k, (hh + 1) * head_k)
            vcols = slice(hh * head_v, (hh + 1) * head_v)
            chains.append((qf_ref, kf_ref, gf_ref, v_ref, tf_ref, mf_ref, s_ref.at[0, hh], of_ref, rows_f,
                           kcols, vcols, True))
            chains.append((qb_ref, kb_ref, gb_ref, v_ref, tb_ref, mb_ref, s_ref.at[1, hh], ob_ref, rows_b,
                           kcols, vcols, False))
        _scan_chunks(chains)
        return carry

    lax.fori_loop(0, n, body, 0)

    if has_sout:
        sout_ref[...] = s_ref[...]

    def norm_body(ci, carry):
        rows = pl.ds(pl.multiple_of(ci * CHUNK, CHUNK), CHUNK)
        for hh in range(hps):
            vcols = slice(hh * head_v, (hh + 1) * head_v)
            o = of_ref[rows, vcols] + ob_ref[rows, vcols]
            r = r_ref[rows, vcols].astype(F32)
            o_ref[rows, vcols] = (_rms(o) * gn_ref[...] * (r * jax.nn.sigmoid(r))).astype(BF16)
        return carry

    lax.fori_loop(0, n, norm_body, 0)


def _gla_scan(p, g, gnorm, s0, *, t, nb, row_block0, heads, hps, head_k, head_v, want_state):
    assert heads % hps == 0
    nhb = heads // hps
    t_f, t_b, m_f, m_b = _scan_constants()
    qk_blk = lambda off: pl.BlockSpec((t, hps * head_k), lambda b, h: (row_block0 + b, off * nhb + h))
    v_off = 4 * heads * head_k // (hps * head_v)
    state_blk = pl.BlockSpec((None, None, 2, hps, head_k, head_v), lambda b, h: (b, 0, 0, h, 0, 0))
    in_specs = [
        qk_blk(0), qk_blk(1), qk_blk(2), qk_blk(3),
        pl.BlockSpec((t, hps * head_v), lambda b, h: (row_block0 + b, v_off + h)),
        pl.BlockSpec((t, hps * head_v), lambda b, h: (row_block0 + b, v_off + nhb + h)),
        pl.BlockSpec((t, hps * head_k), lambda b, h: (row_block0 + b, h)),
        pl.BlockSpec((t, hps * head_k), lambda b, h: (row_block0 + b, nhb + h)),
    ]
    args = [p, p, p, p, p, p, g, g]
    if s0 is not None:
        in_specs.append(state_blk)
        args.append(s0)
    in_specs += [
        pl.BlockSpec(t_f.shape, lambda b, h: (0, 0)),
        pl.BlockSpec(t_b.shape, lambda b, h: (0, 0)),
        pl.BlockSpec(m_f.shape, lambda b, h: (0, 0, 0)),
        pl.BlockSpec(m_b.shape, lambda b, h: (0, 0, 0)),
        pl.BlockSpec((1, head_v), lambda b, h: (0, 0)),
    ]
    args += [t_f, t_b, m_f, m_b, gnorm]
    out_specs = [pl.BlockSpec((t, hps * head_v), lambda b, h: (b, h))]
    out_shape = [jax.ShapeDtypeStruct((nb * t, heads * head_v), BF16)]
    if want_state:
        out_specs.append(state_blk)
        out_shape.append(jax.ShapeDtypeStruct((nb, 1, 2, heads, head_k, head_v), F32))
    return pl.pallas_call(
        functools.partial(_gla_scan_kernel, has_s0=s0 is not None, has_sout=want_state, head_k=head_k,
                          head_v=head_v),
        grid=(nb, nhb),
        in_specs=in_specs,
        out_specs=out_specs,
        out_shape=out_shape,
        scratch_shapes=[
            pltpu.VMEM((2, hps, head_k, head_v), F32),
            pltpu.VMEM((t, hps * head_v), F32),
            pltpu.VMEM((t, hps * head_v), F32),
        ],
        compiler_params=_params("arbitrary", "arbitrary"),
        name="gla_scan",
    )(*args)


def kernel(x_prompt, x_sample, state_gla, c, c_ctx, ada_w, ada_b, norm_pre, norm_post, ffn_w_gate_up,
           ffn_w_down, fourier_w, gla_w_in, gla_w_gate_up, gla_b_gate, gla_norm, gla_w_out):
    batch, seq, d = x_prompt.shape
    dec_batch, dec_seq, _ = x_sample.shape
    depth = ada_w.shape[0]
    heads, head_k, head_v = state_gla.shape[3:]
    dk_tot = heads * head_k
    dv_tot = heads * head_v
    rank = gla_w_gate_up.shape[2]
    d_ff = ffn_w_down.shape[2]
    ctx_rows = batch * seq
    assert ctx_rows % dec_seq == 0 and dec_seq % seq == 0 and 1 + dec_batch <= SUBLANES
    n_ctx_groups = ctx_rows // dec_seq
    grp = dict(group_rows=dec_seq, n_ctx_groups=n_ctx_groups)
    tm_small = min(dec_seq, 512)
    tm_ffn = min(dec_seq, 1024)
    tf = 512 if d_ff % 512 == 0 else d_ff

    cond8 = jnp.zeros((SUBLANES, d), F32).at[0].set(c_ctx).at[1:1 + dec_batch].set(c)
    mods = _modulation(cond8, ada_w, ada_b)[:, :1 + dec_batch].reshape(depth, 1 + dec_batch, N_MOD, d)

    wgu_bf = ffn_w_gate_up.astype(BF16)
    wd_bf = ffn_w_down.astype(BF16)
    fw_bf = fourier_w.astype(BF16)
    n_main = 4 * dk_tot + 2 * dv_tot
    win_bf = gla_w_in[:, :, :n_main].astype(BF16)
    wlr_bf = gla_w_in[:, :, n_main:].astype(BF16)
    wout_bf = gla_w_out.astype(BF16)

    xs = (x_prompt.reshape(ctx_rows, d), x_sample.reshape(dec_batch * dec_seq, d))
    states = []
    for l in range(depth):
        mod = mods[l]
        j = l // 2
        npre = lambda s: norm_pre[l, s][None, :]
        npost = lambda s: norm_post[l, s][None, :]
        ffn = functools.partial(_ffn, mod=mod, wgu=wgu_bf, wd=wd_bf, layer=l, tf=tf, **grp)
        if len(xs) == 2:
            x = ffn(xs, gpre=npre(0), gpost=npost(0), slot=0, k=0, tm=tm_small, x_buffers=None)
        else:
            x = ffn(xs, gpre=npre(0), gpost=npost(0), slot=0, k=0, tm=tm_ffn, x_buffers=1)
        if l % 2 == 0:
            f_ctx = _fourier(x, mod, npre(1), t=seq, nb=batch, row_block0=0, mod0=lambda b: 0, k=1)
            f_lat = _fourier(x, mod, npre(1), t=dec_seq, nb=dec_batch, row_block0=n_ctx_groups,
                             mod0=lambda b: 1 + b, k=1)
            x = _dense_post((f_ctx, f_lat), fw_bf, x, mod, npost(1), widx=j, k=1, tm=tm_small, **grp)
        else:
            zero = jnp.zeros((rank, dk_tot), F32)
            wz = jnp.concatenate([jnp.concatenate([gla_w_gate_up[j, 0], zero], axis=1),
                                  jnp.concatenate([zero, gla_w_gate_up[j, 1]], axis=1)], axis=0).astype(BF16)
            bz = gla_b_gate[j].reshape(1, 2 * dk_tot)
            colscale = jnp.ones((n_main,), F32).at[:dk_tot].set(head_k ** -0.5)
            colscale = colscale.at[2 * dk_tot:3 * dk_tot].set(head_k ** -0.5).reshape(1, n_main)
            p, g = _gla_in(x, mod, npre(1), win_bf[j], wlr_bf[j], wz, bz, colscale, k=1, tm=tm_small,
                           n_steps=4, head_k=head_k, dk_tot=dk_tot, **grp)
            gn = gla_norm[j][None, :]
            scan = functools.partial(_gla_scan, p, g, gn, heads=heads, head_k=head_k, head_v=head_v)
            o_ctx, st = scan(None, t=seq, nb=batch, row_block0=0, hps=heads, want_state=True)
            (o_lat,) = scan(state_gla[:, j:j + 1], t=dec_seq, nb=dec_batch, row_block0=n_ctx_groups,
                            hps=min(heads, max(1, 2 * heads * seq // dec_seq)), want_state=False)
            states.append(st)
            x = _dense_post((o_ctx, o_lat), wout_bf, x, mod, npost(1), widx=j, k=1, tm=tm_small, **grp)
        xs = (ffn((x,), gpre=npre(2), gpost=npost(2), slot=1, k=2, tm=tm_ffn, x_buffers=1),)

    x = xs[0]
    y_prompt = x[:ctx_rows].reshape(batch, seq, d)
    y_sample = x[ctx_rows:].reshape(dec_batch, dec_seq, d)
    new_state = jnp.concatenate(states, axis=1)
    return (y_prompt, y_sample, new_state)
```

```python
import functools

import numpy as np
import jax
import jax.numpy as jnp
from jax import lax
from jax.experimental import pallas as pl
from jax.experimental.pallas import tpu as pltpu

F32 = jnp.float32
BF16 = jnp.bfloat16

EPS = 1e-6
N_MOD = 9
GRID_W = 64
FOURIER_GROUPS = 4
GATE_TAU = 16.0
CHUNK = 64
ROPE_BASE = 10000.0
LOG2_E = 1.4426950408889634
N_LEVELS = 6
assert CHUNK == 1 << N_LEVELS

V7X_VMEM_BYTES = 64 * 1024 * 1024
MIB = 1024 * 1024
LANES = 128
SUBLANES = 8


def _params(*sem, vmem_bytes):
    request = (int(vmem_bytes) // MIB + 2) * MIB
    assert request < V7X_VMEM_BYTES, request
    return pltpu.CompilerParams(dimension_semantics=sem, vmem_limit_bytes=request)


def _dot(a, b):
    return jnp.dot(a, b, preferred_element_type=F32)


def _dot_nt(a, b):
    return lax.dot_general(a, b, (((1,), (1,)), ((), ())), preferred_element_type=F32)


def _dot_tn(a, b):
    return lax.dot_general(a, b, (((0,), (0,)), ((), ())), preferred_element_type=F32)


def _rms(x):
    return x * lax.rsqrt(jnp.mean(x * x, axis=-1, keepdims=True) + EPS)


def _pre_mod(x, gpre, mod_ref, k):
    gs = gpre * (1.0 + mod_ref[3 * k + 1:3 * k + 2, :])
    return _rms(x) * gs + mod_ref[3 * k:3 * k + 1, :]


def _post_add(x, out, gpost, mod_ref, k, res_w):
    gg = (res_w * mod_ref[3 * k + 2:3 * k + 3, :]) * gpost
    return x + _rms(out) * gg


ROW_BLOCK = 128


def _for_row_blocks(n_rows, fn):
    block = min(ROW_BLOCK, n_rows)

    def body(r, carry):
        fn(pl.ds(pl.multiple_of(r * block, block), block))
        return carry

    lax.fori_loop(0, n_rows // block, body, 0)


def _mod_index(i, tm, group_rows, n_ctx_groups):
    return jnp.maximum(i * tm // group_rows - (n_ctx_groups - 1), 0)


def _on_tile_ref(refs, i, n_ctx_tiles, fn):
    if len(refs) == 1:
        fn(refs[0])
    else:
        pl.when(i < n_ctx_tiles)(lambda: fn(refs[0]))
        pl.when(i >= n_ctx_tiles)(lambda: fn(refs[1]))


def _split_row_specs(arrays, tm, width, n_ctx_tiles, ngrid, buffers=None):
    kw = {} if buffers is None else dict(pipeline_mode=pl.Buffered(buffers))
    if ngrid == 1:
        wrap = lambda f: (lambda i: f(i))
    else:
        wrap = lambda f: (lambda i, j: f(i))
    if len(arrays) == 1:
        return [pl.BlockSpec((tm, width), wrap(lambda i: (i, 0)), **kw)]
    return [
        pl.BlockSpec((tm, width), wrap(lambda i: (jnp.minimum(i, n_ctx_tiles - 1), 0)), **kw),
        pl.BlockSpec((tm, width), wrap(lambda i: (jnp.maximum(i - n_ctx_tiles, 0), 0)), **kw),
    ]


def _mod_kernel(c_ref, w_ref, b_ref, o_ref):
    c = c_ref[...]
    s = (c * jax.nn.sigmoid(c)).astype(BF16)
    o_ref[...] = _dot(s, w_ref[...].astype(BF16)) + b_ref[...]


def _modulation(cond8, ada_w, ada_b):
    depth, d, n = ada_w.shape
    tn = min(n, 1024)
    return pl.pallas_call(
        _mod_kernel,
        grid=(depth, n // tn),
        in_specs=[
            pl.BlockSpec((SUBLANES, d), lambda l, j: (0, 0)),
            pl.BlockSpec((None, d, tn), lambda l, j: (l, 0, j)),
            pl.BlockSpec((None, 1, tn), lambda l, j: (l, 0, j)),
        ],
        out_specs=pl.BlockSpec((None, SUBLANES, tn), lambda l, j: (l, 0, j)),
        out_shape=jax.ShapeDtypeStruct((depth, SUBLANES, n), F32),
        compiler_params=_params("arbitrary", "arbitrary", vmem_bytes=d * tn * (2 * 4 + 2) + 2 * MIB),
        name="modulation",
    )(cond8, ada_w, ada_b.reshape(depth, 1, n))


def _on_tile_part(parts, tile, n_ctx_tiles, fn):
    if len(parts) == 1:
        fn(parts[0], tile)
    else:
        pl.when(tile < n_ctx_tiles)(lambda: fn(parts[0], tile))
        pl.when(tile >= n_ctx_tiles)(lambda: fn(parts[1], tile - n_ctx_tiles))


def _ffn_kernel(*refs, n_x, n_out, n_ctx_tiles, k, res_w):
    x_hbm = refs[:n_x]
    mod_ref, gpre_ref, gpost_ref, wg_ref, wu_ref, wd_ref = refs[n_x:n_x + 6]
    o_hbm = refs[n_x + 6:n_x + 6 + n_out]
    h_ref, x_buf, acc_ref, out_sem = refs[n_x + 6 + n_out:]
    i = pl.program_id(0)
    j = pl.program_id(1)
    tm = x_buf.shape[0]

    def rows_of(ref, tile):
        return ref.at[pl.ds(pl.multiple_of(tile * tm, tm), tm), :]

    def out_copy(ref, tile):
        return pltpu.make_async_copy(acc_ref, rows_of(ref, tile), out_sem)

    @pl.when(j == 0)
    def _():
        _on_tile_part(x_hbm, i, n_ctx_tiles, lambda ref, t: pltpu.sync_copy(rows_of(ref, t), x_buf))
        def prologue(rows):
            h_ref[rows, :] = _pre_mod(x_buf[rows, :], gpre_ref[...], mod_ref, k).astype(BF16)

        _for_row_blocks(tm, prologue)

        @pl.when(i > 0)
        def _():
            _on_tile_part(o_hbm, i - 1, n_ctx_tiles, lambda ref, t: out_copy(ref, t).wait())

        acc_ref[...] = jnp.zeros_like(acc_ref)

    h = h_ref[...]
    half = wg_ref.shape[1] // 2
    for c0 in (0, half):
        g = _dot(h, wg_ref[:, c0:c0 + half].astype(BF16))
        u = _dot(h, wu_ref[:, c0:c0 + half].astype(BF16))
        a = (g * jax.nn.sigmoid(g) * u).astype(BF16)
        acc_ref[...] += _dot(a, wd_ref[c0:c0 + half, :].astype(BF16))

    @pl.when(j == pl.num_programs(1) - 1)
    def _():
        def epilogue(rows):
            acc_ref[rows, :] = _post_add(x_buf[rows, :], acc_ref[rows, :], gpost_ref[...], mod_ref, k, res_w)

        _for_row_blocks(tm, epilogue)
        _on_tile_part(o_hbm, i, n_ctx_tiles, lambda ref, t: out_copy(ref, t).start())

        @pl.when(i == pl.num_programs(0) - 1)
        def _():
            _on_tile_part(o_hbm, i, n_ctx_tiles, lambda ref, t: out_copy(ref, t).wait())


def _ffn(xs, mod, gpre, gpost, wgu, wd, *, layer, slot, k, tm, tf, split_out, group_rows, n_ctx_groups):
    m = sum(x.shape[0] for x in xs)
    d = xs[0].shape[1]
    nf = wd.shape[2] // tf
    n_ctx_tiles = n_ctx_groups * group_rows // tm
    out_rows = (n_ctx_tiles * tm, m - n_ctx_tiles * tm) if split_out else (m,)
    midx = functools.partial(_mod_index, tm=tm, group_rows=group_rows, n_ctx_groups=n_ctx_groups)
    hbm = pl.BlockSpec(memory_space=pl.ANY)
    return pl.pallas_call(
        functools.partial(_ffn_kernel, n_x=len(xs), n_out=len(out_rows), n_ctx_tiles=n_ctx_tiles, k=k,
                          res_w=0.5),
        grid=(m // tm, nf),
        in_specs=[hbm] * len(xs) + [
            pl.BlockSpec((None, N_MOD, d), lambda i, j: (midx(i), 0, 0)),
            pl.BlockSpec((1, d), lambda i, j: (0, 0)),
            pl.BlockSpec((1, d), lambda i, j: (0, 0)),
            pl.BlockSpec((None, None, d, tf), lambda i, j: (layer, slot, 0, j)),
            pl.BlockSpec((None, None, d, tf), lambda i, j: (layer, slot, 0, j + nf)),
            pl.BlockSpec((None, None, tf, d), lambda i, j: (layer, slot, j, 0)),
        ],
        out_specs=[hbm] * len(out_rows),
        out_shape=[jax.ShapeDtypeStruct((r, d), F32) for r in out_rows],
        scratch_shapes=[
            pltpu.VMEM((tm, d), BF16),
            pltpu.VMEM((tm, d), F32),
            pltpu.VMEM((tm, d), F32),
            pltpu.SemaphoreType.DMA(()),
        ],
        compiler_params=_params("arbitrary", "arbitrary",
                                vmem_bytes=tm * d * (2 + 4 + 4) + 2 * 3 * d * tf * 4
                                + (tf // 2) * (tm * (4 + 4 + 2) + 3 * d * 2) + 2 * MIB),
        name="ffn",
    )(*xs, mod, gpre, gpost, wgu, wgu, wd)


def _dense_post_kernel(*refs, n_a, n_ctx_tiles, k, res_w):
    a_refs = refs[:n_a]
    w_ref, x_ref, mod_ref, gpost_ref, o_ref = refs[n_a:]

    def body(a_ref):
        o_ref[...] = _dot(a_ref[...], w_ref[...])

        def block(rows):
            o_ref[rows, :] = _post_add(x_ref[rows, :], o_ref[rows, :], gpost_ref[...], mod_ref, k, res_w)

        _for_row_blocks(o_ref.shape[0], block)

    _on_tile_ref(a_refs, pl.program_id(0), n_ctx_tiles, body)


def _dense_post(a_parts, w, x, mod, gpost, *, widx, k, tm, group_rows, n_ctx_groups):
    m, d = x.shape
    kk = a_parts[0].shape[1]
    n_ctx_tiles = n_ctx_groups * group_rows // tm
    midx = functools.partial(_mod_index, tm=tm, group_rows=group_rows, n_ctx_groups=n_ctx_groups)
    return pl.pallas_call(
        functools.partial(_dense_post_kernel, n_a=len(a_parts), n_ctx_tiles=n_ctx_tiles, k=k, res_w=1.0),
        grid=(m // tm,),
        in_specs=_split_row_specs(a_parts, tm, kk, n_ctx_tiles, 1) + [
            pl.BlockSpec((None, kk, d), lambda i: (widx, 0, 0)),
            pl.BlockSpec((tm, d), lambda i: (i, 0)),
            pl.BlockSpec((None, N_MOD, d), lambda i: (midx(i), 0, 0)),
            pl.BlockSpec((1, d), lambda i: (0, 0)),
        ],
        out_specs=pl.BlockSpec((tm, d), lambda i: (i, 0)),
        out_shape=jax.ShapeDtypeStruct((m, d), F32),
        compiler_params=_params("arbitrary", vmem_bytes=2 * (len(a_parts) * tm * kk * 2 + kk * d * 2
                                                             + 2 * tm * d * 4) + 2 * tm * d * 4),
        name="dense_post",
    )(*a_parts, w, x, mod, gpost)


def _dft_tables(t, w):
    def cs(n):
        kn = np.outer(np.arange(n), np.arange(n)) % n
        ang = 2.0 * np.pi * kn.astype(np.float64) / n
        return np.cos(ang) / np.sqrt(n), np.sin(ang) / np.sqrt(n)

    ct, st = cs(t)
    cw, sw = cs(w)
    as_bf16 = lambda a: jnp.asarray(a, dtype=F32).astype(BF16)
    return as_bf16(np.concatenate([ct, st], axis=0)), as_bf16(cw), as_bf16(-sw)


def _fourier_kernel(x_ref, mod_ref, gpre_ref, cst_ref, cw_ref, swn_ref, o_ref, *, k):
    t = x_ref.shape[0]
    w = cw_ref.shape[0]
    h = _pre_mod(x_ref[...], gpre_ref[...], mod_ref, k).astype(BF16)
    cst = cst_ref[...]
    for g in range(FOURIER_GROUPS):
        p = _dot(cst, h[:, g * w:(g + 1) * w]).astype(BF16)
        f = _dot(p[:t], cw_ref[...]) + _dot(p[t:], swn_ref[...])
        o_ref[:, g * w:(g + 1) * w] = f.astype(BF16)


def _fourier(x, mod, gpre, *, t, nb, row_block0, mod0, k):
    d = x.shape[1]
    w = d // FOURIER_GROUPS
    cst, cw, swn = _dft_tables(t, w)
    return pl.pallas_call(
        functools.partial(_fourier_kernel, k=k),
        grid=(nb,),
        in_specs=[
            pl.BlockSpec((t, d), lambda b: (row_block0 + b, 0)),
            pl.BlockSpec((None, N_MOD, d), lambda b: (mod0(b), 0, 0)),
            pl.BlockSpec((1, d), lambda b: (0, 0)),
            pl.BlockSpec((2 * t, t), lambda b: (0, 0)),
            pl.BlockSpec((w, w), lambda b: (0, 0)),
            pl.BlockSpec((w, w), lambda b: (0, 0)),
        ],
        out_specs=pl.BlockSpec((t, d), lambda b: (b, 0)),
        out_shape=jax.ShapeDtypeStruct((nb * t, d), BF16),
        compiler_params=_params("arbitrary", vmem_bytes=2 * (t * d * 4 + 2 * t * t * 2 + 2 * w * w * 2
                                                             + t * d * 2) + t * d * 2 + 2 * t * w * 8),
        name="fourier",
    )(x, mod, gpre, cst, cw, swn)


def _rope_tables(t, head_k):
    rows = t // GRID_W
    pairs = head_k // 4
    row = np.repeat(np.arange(rows), GRID_W).astype(np.float64)
    col = np.tile(np.arange(GRID_W), rows).astype(np.float64)
    inv = ROPE_BASE ** (-np.arange(pairs, dtype=np.float64) / pairs)
    ang = np.concatenate([row[:, None] * inv, col[:, None] * inv], axis=-1)
    cos, sin = np.cos(ang), np.sin(ang)
    ce = np.repeat(cos, 2, axis=-1)
    se = np.stack([-sin, sin], axis=-1).reshape(t, head_k)
    return jnp.asarray(ce, dtype=F32), jnp.asarray(se, dtype=F32)


def _gla_in_kernel(x_ref, mod_ref, gpre_ref, w_ref, wlr_ref, wz_ref, bz_ref, cs_ref, ce_ref, se_ref,
                   p_ref, g_ref, h_ref, lr_ref, *, k, n_ctx_tiles, n_qk_steps, head_k):
    i = pl.program_id(0)
    j = pl.program_id(1)

    @pl.when(j == 0)
    def _():
        def prologue(rows):
            h = _pre_mod(x_ref[rows, :], gpre_ref[...], mod_ref, k).astype(BF16)
            h_ref[rows, :] = h
            lr_ref[rows, :] = _dot(h, wlr_ref[...]).astype(BF16)

        _for_row_blocks(x_ref.shape[0], prologue)

    def step(rope):
        z = _dot(lr_ref[...], wz_ref[...]) + bz_ref[...]
        g_ref[...] = (jnp.minimum(z, 0.0) - jnp.log1p(jnp.exp(-jnp.abs(z)))) * (LOG2_E / GATE_TAU)
        h = h_ref[...]
        for s in range(w_ref.shape[1] // head_k):
            cols = slice(s * head_k, (s + 1) * head_k)
            acc = _dot(h, w_ref[:, cols].astype(BF16)) * cs_ref[:, cols]
            if rope:
                lane = lax.broadcasted_iota(jnp.int32, acc.shape, 1)
                partner = jnp.where(lane % 2 == 0, pltpu.roll(acc, head_k - 1, 1), pltpu.roll(acc, 1, 1))
                acc = acc * ce_ref[...] + partner * se_ref[...]
            p_ref[:, cols] = acc.astype(BF16)

    rope = jnp.logical_and(i >= n_ctx_tiles, j < n_qk_steps)
    pl.when(rope)(lambda: step(True))
    pl.when(jnp.logical_not(rope))(lambda: step(False))


def _gla_in(x, mod, gpre, w_in, w_lr, wz, bz, colscale, *, widx, n_main, k, tm, n_steps, head_k, dk_tot,
            group_rows, n_ctx_groups):
    m, d = x.shape
    n = n_main
    tn = n // n_steps
    gcols = 2 * dk_tot // n_steps
    assert (4 * dk_tot) % tn == 0 and tn % head_k == 0 and gcols % LANES == 0
    n_ctx_tiles = n_ctx_groups * group_rows // tm
    tiles_per_group = group_rows // tm
    ce, se = _rope_tables(group_rows, head_k)
    midx = functools.partial(_mod_index, tm=tm, group_rows=group_rows, n_ctx_groups=n_ctx_groups)

    def tab_idx(i, j):
        return (jnp.where(i >= n_ctx_tiles, (i - n_ctx_tiles) % tiles_per_group, 0), 0)

    return pl.pallas_call(
        functools.partial(_gla_in_kernel, k=k, n_ctx_tiles=n_ctx_tiles, n_qk_steps=4 * dk_tot // tn,
                          head_k=head_k),
        grid=(m // tm, n_steps),
        in_specs=[
            pl.BlockSpec((tm, d), lambda i, j: (i, 0)),
            pl.BlockSpec((None, N_MOD, d), lambda i, j: (midx(i), 0, 0)),
            pl.BlockSpec((1, d), lambda i, j: (0, 0)),
            pl.BlockSpec((None, d, tn), lambda i, j: (widx, 0, j)),
            pl.BlockSpec(w_lr.shape, lambda i, j: (0, 0)),
            pl.BlockSpec((wz.shape[0], gcols), lambda i, j: (0, j)),
            pl.BlockSpec((1, gcols), lambda i, j: (0, j)),
            pl.BlockSpec((1, tn), lambda i, j: (0, j)),
            pl.BlockSpec((tm, head_k), tab_idx),
            pl.BlockSpec((tm, head_k), tab_idx),
        ],
        out_specs=[
            pl.BlockSpec((tm, tn), lambda i, j: (i, j)),
            pl.BlockSpec((tm, gcols), lambda i, j: (i, j)),
        ],
        out_shape=[
            jax.ShapeDtypeStruct((m, n), BF16),
            jax.ShapeDtypeStruct((m, 2 * dk_tot), F32),
        ],
        scratch_shapes=[pltpu.VMEM((tm, d), BF16), pltpu.VMEM((tm, w_lr.shape[1]), BF16)],
        compiler_params=_params("arbitrary", "arbitrary",
                                vmem_bytes=2 * (tm * d * 4 + d * tn * 4 + tm * tn * 2 + tm * gcols * 4
                                                + 2 * tm * head_k * 4) + tm * d * 2
                                + 2 * head_k * (tm * 4 + d * 2) + 2 * tm * gcols * 4 + 2 * MIB),
        name="gla_in",
    )(x, mod, gpre, w_in, w_lr, wz, bz, colscale, ce, se)


def _scan_constants():
    c = CHUNK
    idx = np.arange(c)
    tri = (idx[None, :] <= idx[:, None]).astype(np.float64)
    masks = [np.eye(c)]
    for lev in range(N_LEVELS):
        h = 1 << lev
        start = idx // (2 * h) * (2 * h)
        upper = (idx - start) >= h
        same = start[:, None] == start[None, :]
        masks.append((same & upper[:, None] & ~upper[None, :]).astype(np.float64))
    m_f = np.stack(masks, axis=0)
    m_b = m_f[:, ::-1, ::-1]
    t_f = np.concatenate([tri, tri], axis=1)
    t_b = np.concatenate([tri[::-1, ::-1]] * 2, axis=1)
    return (jnp.asarray(t_f, dtype=BF16), jnp.asarray(t_b, dtype=BF16),
            jnp.asarray(m_f, dtype=F32), jnp.asarray(m_b, dtype=F32))


def _level_exponents(b, g, fwd):
    c, kk = b.shape
    tile = (c // SUBLANES, SUBLANES, kk)
    b8 = b.reshape(tile)
    r8 = lax.broadcasted_iota(jnp.int32, tile, 1)
    out = [jnp.where(r8 % 2 == (1 if fwd else 0), g.reshape(tile), 0.0).reshape(c, kk)]
    lo, hi = (1, 5) if fwd else (2, 6)
    mids = [jnp.where(r8 < 4, b8[:, lo:lo + 1, :], b8[:, hi:hi + 1, :]), b8[:, 3:4, :] if fwd else b8[:, 4:5, :]]
    for lev, mid in zip((1, 2), mids):
        h = 1 << lev
        second_half = (r8 % (2 * h)) >= h
        sign = jnp.where(second_half if fwd else jnp.logical_not(second_half), 1.0, -1.0)
        out.append(((b8 - mid) * sign).reshape(c, kk))
    for lev in range(3, N_LEVELS):
        h = 1 << lev
        bb = b.reshape(c // (2 * h), 2 * h, kk)
        first, second = bb[:, :h, :], bb[:, h:, :]
        if fwd:
            mid = bb[:, h - 1:h, :]
            t = jnp.concatenate([mid - first, second - mid], axis=1)
        else:
            mid = bb[:, h:h + 1, :]
            t = jnp.concatenate([first - mid, mid - second], axis=1)
        out.append(t.reshape(c, kk))
    return out


def _scan_chunks(chains):
    c = CHUNK

    st1 = []
    for q_ref, k_ref, g_ref, v_ref, tri_ref, m_ref, s_ref, o_ref, rows, kcols, vcols, fwd in chains:
        g = g_ref[rows, kcols]
        g_hi = g.astype(BF16)
        g_lo = (g - g_hi.astype(F32)).astype(BF16)
        b = _dot(tri_ref[...], jnp.concatenate([g_hi, g_lo], axis=0))
        diag = _dot_nt(q_ref[rows, kcols], k_ref[rows, kcols])
        st1.append((g, b, diag))

    st2 = []
    for (q_ref, k_ref, g_ref, v_ref, tri_ref, m_ref, s_ref, o_ref, rows, kcols, vcols, fwd), (g, b, diag) in zip(
            chains, st1):
        q_bf = q_ref[rows, kcols]
        k_bf = k_ref[rows, kcols]
        scores = m_ref[0] * diag
        for lev, t in enumerate(_level_exponents(b, g, fwd)):
            e = jnp.exp2(t).astype(BF16)
            scores += m_ref[lev + 1] * _dot_nt(q_bf * e, k_bf * e)
        st2.append(scores)

    for (q_ref, k_ref, g_ref, v_ref, tri_ref, m_ref, s_ref, o_ref, rows, kcols, vcols, fwd), (g, b, diag), scores in zip(
            chains, st1, st2):
        qe = q_ref[rows, kcols] * jnp.exp2(b).astype(BF16)
        o_ref[rows, vcols] = _dot(qe, s_ref[...].astype(BF16)) + _dot(scores.astype(BF16), v_ref[rows, vcols])

    for (q_ref, k_ref, g_ref, v_ref, tri_ref, m_ref, s_ref, o_ref, rows, kcols, vcols, fwd), (g, b, diag) in zip(
            chains, st1):
        last = c - 1 if fwd else 0
        b_last = b[last:last + 1]
        kt = k_ref[rows, kcols] * jnp.exp2(b_last - b).astype(BF16)
        e_col = jnp.broadcast_to(jnp.exp2(b_last), (LANES, b.shape[1])).T
        s = s_ref[...]
        decay = jnp.concatenate([e_col] * (s.shape[1] // LANES), axis=1)
        s_ref[...] = decay * s + _dot_tn(kt, v_ref[rows, vcols])


def _gla_scan_kernel(*refs, has_s0, has_sout, head_k, head_v):
    qf_ref, kf_ref, qb_ref, kb_ref, v_ref, r_ref, gf_ref, gb_ref = refs[:8]
    pos = 8
    if has_s0:
        s0_ref = refs[pos]
        pos += 1
    tf_ref, tb_ref, mf_ref, mb_ref, gn_ref = refs[pos:pos + 5]
    pos += 5
    o_ref = refs[pos]
    pos += 1
    if has_sout:
        sout_ref = refs[pos]
        pos += 1
    s_ref, of_ref, ob_ref = refs[pos:pos + 3]

    t = v_ref.shape[0]
    n = t // CHUNK
    hps = v_ref.shape[1] // head_v
    if has_s0:
        s_ref[...] = s0_ref[...]
    else:
        s_ref[...] = jnp.zeros_like(s_ref)

    def body(ci, carry):
        rows_f = pl.ds(pl.multiple_of(ci * CHUNK, CHUNK), CHUNK)
        rows_b = pl.ds(pl.multiple_of((n - 1 - ci) * CHUNK, CHUNK), CHUNK)
        chains = []
        for hh in range(hps):
            kcols = slice(hh * head_k, (hh + 1) * head_k)
            vcols = slice(hh * head_v, (hh + 1) * head_v)
            chains.append((qf_ref, kf_ref, gf_ref, v_ref, tf_ref, mf_ref, s_ref.at[0, hh], of_ref, rows_f,
                           kcols, vcols, True))
            chains.append((qb_ref, kb_ref, gb_ref, v_ref, tb_ref, mb_ref, s_ref.at[1, hh], ob_ref, rows_b,
                           kcols, vcols, False))
        _scan_chunks(chains)
        return carry

    lax.fori_loop(0, n, body, 0)

    if has_sout:
        sout_ref[...] = s_ref[...]

    def norm_body(ci, carry):
        rows = pl.ds(pl.multiple_of(ci * CHUNK, CHUNK), CHUNK)
        for hh in range(hps):
            vcols = slice(hh * head_v, (hh + 1) * head_v)
            o = of_ref[rows, vcols] + ob_ref[rows, vcols]
            r = r_ref[rows, vcols].astype(F32)
            o_ref[rows, vcols] = (_rms(o) * gn_ref[...] * (r * jax.nn.sigmoid(r))).astype(BF16)
        return carry

    lax.fori_loop(0, n, norm_body, 0)


def _gla_scan(p, g, gnorm, s0, *, t, nb, row_block0, heads, hps, head_k, head_v, want_state):
    assert heads % hps == 0
    nhb = heads // hps
    t_f, t_b, m_f, m_b = _scan_constants()
    qk_blk = lambda off: pl.BlockSpec((t, hps * head_k), lambda b, h: (row_block0 + b, off * nhb + h))
    v_off = 4 * heads * head_k // (hps * head_v)
    state_blk = pl.BlockSpec((None, None, 2, hps, head_k, head_v), lambda b, h: (b, 0, 0, h, 0, 0))
    in_specs = [
        qk_blk(0), qk_blk(1), qk_blk(2), qk_blk(3),
        pl.BlockSpec((t, hps * head_v), lambda b, h: (row_block0 + b, v_off + h)),
        pl.BlockSpec((t, hps * head_v), lambda b, h: (row_block0 + b, v_off + nhb + h)),
        pl.BlockSpec((t, hps * head_k), lambda b, h: (row_block0 + b, h)),
        pl.BlockSpec((t, hps * head_k), lambda b, h: (row_block0 + b, nhb + h)),
    ]
    args = [p, p, p, p, p, p, g, g]
    if s0 is not None:
        in_specs.append(state_blk)
        args.append(s0)
    in_specs += [
        pl.BlockSpec(t_f.shape, lambda b, h: (0, 0)),
        pl.BlockSpec(t_b.shape, lambda b, h: (0, 0)),
        pl.BlockSpec(m_f.shape, lambda b, h: (0, 0, 0)),
        pl.BlockSpec(m_b.shape, lambda b, h: (0, 0, 0)),
        pl.BlockSpec((1, head_v), lambda b, h: (0, 0)),
    ]
    args += [t_f, t_b, m_f, m_b, gnorm]
    out_specs = [pl.BlockSpec((t, hps * head_v), lambda b, h: (b, h))]
    out_shape = [jax.ShapeDtypeStruct((nb * t, heads * head_v), BF16)]
    if want_state:
        out_specs.append(state_blk)
        out_shape.append(jax.ShapeDtypeStruct((nb, 1, 2, heads, head_k, head_v), F32))
    return pl.pallas_call(
        functools.partial(_gla_scan_kernel, has_s0=s0 is not None, has_sout=want_state, head_k=head_k,
                          head_v=head_v),
        grid=(nb, nhb),
        in_specs=in_specs,
        out_specs=out_specs,
        out_shape=out_shape,
        scratch_shapes=[
            pltpu.VMEM((2, hps, head_k, head_v), F32),
            pltpu.VMEM((t, hps * head_v), F32),
            pltpu.VMEM((t, hps * head_v), F32),
        ],
        compiler_params=_params(
            "arbitrary", "arbitrary",
            vmem_bytes=2 * t * hps * (4 * head_k * 2 + 3 * head_v * 2 + 2 * head_k * 4)
            + (2 * (int(s0 is not None) + int(want_state)) + 1) * 2 * hps * head_k * head_v * 4
            + 2 * t * hps * head_v * 4 + 8 * MIB),
        name="gla_scan",
    )(*args)


def kernel(x_prompt, x_sample, state_gla, c, c_ctx, ada_w, ada_b, norm_pre, norm_post, ffn_w_gate_up,
           ffn_w_down, fourier_w, gla_w_in, gla_w_gate_up, gla_b_gate, gla_norm, gla_w_out):
    batch, seq, d = x_prompt.shape
    dec_batch, dec_seq, _ = x_sample.shape
    depth = ada_w.shape[0]
    heads, head_k, head_v = state_gla.shape[3:]
    dk_tot = heads * head_k
    dv_tot = heads * head_v
    rank = gla_w_gate_up.shape[2]
    d_ff = ffn_w_down.shape[2]
    ctx_rows = batch * seq
    assert ctx_rows % dec_seq == 0 and dec_seq % seq == 0 and 1 + dec_batch <= SUBLANES
    n_ctx_groups = ctx_rows // dec_seq
    grp = dict(group_rows=dec_seq, n_ctx_groups=n_ctx_groups)
    tm_small = min(dec_seq, 512)
    tm_big = min(dec_seq, 1024)
    tf = 512 if d_ff % 512 == 0 else d_ff

    cond8 = jnp.zeros((SUBLANES, d), F32).at[0].set(c_ctx).at[1:1 + dec_batch].set(c)
    mods = _modulation(cond8, ada_w, ada_b)[:, :1 + dec_batch].reshape(depth, 1 + dec_batch, N_MOD, d)

    fw_bf = fourier_w.astype(BF16)
    n_main = 4 * dk_tot + 2 * dv_tot
    wlr_bf = gla_w_in[:, :, n_main:].astype(BF16)
    wout_bf = gla_w_out.astype(BF16)

    xs = (x_prompt.reshape(ctx_rows, d), x_sample.reshape(dec_batch * dec_seq, d))
    states = []
    for l in range(depth):
        mod = mods[l]
        j = l // 2
        npre = lambda s: norm_pre[l, s][None, :]
        npost = lambda s: norm_post[l, s][None, :]
        ffn = functools.partial(_ffn, mod=mod, wgu=ffn_w_gate_up, wd=ffn_w_down, layer=l, tf=tf, tm=tm_big,
                                **grp)
        (x,) = ffn(xs, gpre=npre(0), gpost=npost(0), slot=0, k=0, split_out=False)
        if l % 2 == 0:
            f_ctx = _fourier(x, mod, npre(1), t=seq, nb=batch, row_block0=0, mod0=lambda b: 0, k=1)
            f_lat = _fourier(x, mod, npre(1), t=dec_seq, nb=dec_batch, row_block0=n_ctx_groups,
                             mod0=lambda b: 1 + b, k=1)
            x = _dense_post((f_ctx, f_lat), fw_bf, x, mod, npost(1), widx=j, k=1, tm=tm_small, **grp)
        else:
            zero = jnp.zeros((rank, dk_tot), F32)
            wz = jnp.concatenate([jnp.concatenate([gla_w_gate_up[j, 0], zero], axis=1),
                                  jnp.concatenate([zero, gla_w_gate_up[j, 1]], axis=1)], axis=0).astype(BF16)
            bz = gla_b_gate[j].reshape(1, 2 * dk_tot)
            colscale = jnp.ones((n_main,), F32).at[:dk_tot].set(head_k ** -0.5)
            colscale = colscale.at[2 * dk_tot:3 * dk_tot].set(head_k ** -0.5).reshape(1, n_main)
            p, g = _gla_in(x, mod, npre(1), gla_w_in, wlr_bf[j], wz, bz, colscale, widx=j, n_main=n_main, k=1,
                           tm=tm_big, n_steps=8, head_k=head_k, dk_tot=dk_tot, **grp)
            gn = gla_norm[j][None, :]
            scan = functools.partial(_gla_scan, p, g, gn, heads=heads, head_k=head_k, head_v=head_v)
            o_ctx, st = scan(None, t=seq, nb=batch, row_block0=0, hps=heads, want_state=True)
            (o_lat,) = scan(state_gla[:, j:j + 1], t=dec_seq, nb=dec_batch, row_block0=n_ctx_groups,
                            hps=min(heads, max(1, 2 * heads * seq // dec_seq)), want_state=False)
            states.append(st)
            x = _dense_post((o_ctx, o_lat), wout_bf, x, mod, npost(1), widx=j, k=1, tm=tm_small, **grp)
        xs = tuple(ffn((x,), gpre=npre(2), gpost=npost(2), slot=1, k=2, split_out=l == depth - 1))

    y_prompt = xs[0].reshape(batch, seq, d)
    y_sample = xs[1].reshape(dec_batch, dec_seq, d)
    new_state = jnp.concatenate(states, axis=1)
    return (y_prompt, y_sample, new_state)
```

```python
import functools

import numpy as np
import jax
import jax.numpy as jnp
from jax import lax
from jax.experimental import pallas as pl
from jax.experimental.pallas import tpu as pltpu

F32 = jnp.float32
BF16 = jnp.bfloat16

EPS = 1e-6
N_MOD = 9
GRID_W = 64
FOURIER_GROUPS = 4
GATE_TAU = 16.0
CHUNK = 64
ROPE_BASE = 10000.0
LOG2_E = 1.4426950408889634
N_LEVELS = 6
assert CHUNK == 1 << N_LEVELS

V7X_VMEM_BYTES = 64 * 1024 * 1024
MIB = 1024 * 1024
LANES = 128
SUBLANES = 8


def _params(*sem, vmem_bytes):
    request = (int(vmem_bytes) // MIB + 2) * MIB
    assert request < V7X_VMEM_BYTES, request
    return pltpu.CompilerParams(dimension_semantics=sem, vmem_limit_bytes=request)


def _dot(a, b):
    return jnp.dot(a, b, preferred_element_type=F32)


def _dot_nt(a, b):
    return lax.dot_general(a, b, (((1,), (1,)), ((), ())), preferred_element_type=F32)


def _dot_tn(a, b):
    return lax.dot_general(a, b, (((0,), (0,)), ((), ())), preferred_element_type=F32)


def _rms(x):
    return x * lax.rsqrt(jnp.mean(x * x, axis=-1, keepdims=True) + EPS)


def _pre_mod(x, gpre, mod_ref, k):
    gs = gpre * (1.0 + mod_ref[3 * k + 1:3 * k + 2, :])
    return _rms(x) * gs + mod_ref[3 * k:3 * k + 1, :]


def _post_add(x, out, gpost, mod_ref, k, res_w):
    gg = (res_w * mod_ref[3 * k + 2:3 * k + 3, :]) * gpost
    return x + _rms(out) * gg


ROW_BLOCK = 128


def _for_row_blocks(n_rows, fn):
    block = min(ROW_BLOCK, n_rows)

    def body(r, carry):
        fn(pl.ds(pl.multiple_of(r * block, block), block))
        return carry

    lax.fori_loop(0, n_rows // block, body, 0)


def _mod_index(i, tm, group_rows, n_ctx_groups):
    return jnp.maximum(i * tm // group_rows - (n_ctx_groups - 1), 0)


def _on_tile_ref(refs, i, n_ctx_tiles, fn):
    if len(refs) == 1:
        fn(refs[0])
    else:
        pl.when(i < n_ctx_tiles)(lambda: fn(refs[0]))
        pl.when(i >= n_ctx_tiles)(lambda: fn(refs[1]))


def _split_row_specs(arrays, tm, width, n_ctx_tiles, ngrid, buffers=None):
    kw = {} if buffers is None else dict(pipeline_mode=pl.Buffered(buffers))
    if ngrid == 1:
        wrap = lambda f: (lambda i: f(i))
    else:
        wrap = lambda f: (lambda i, j: f(i))
    if len(arrays) == 1:
        return [pl.BlockSpec((tm, width), wrap(lambda i: (i, 0)), **kw)]
    return [
        pl.BlockSpec((tm, width), wrap(lambda i: (jnp.minimum(i, n_ctx_tiles - 1), 0)), **kw),
        pl.BlockSpec((tm, width), wrap(lambda i: (jnp.maximum(i - n_ctx_tiles, 0), 0)), **kw),
    ]


def _mod_kernel(c_ref, w_ref, b_ref, o_ref):
    c = c_ref[...]
    s = (c * jax.nn.sigmoid(c)).astype(BF16)
    o_ref[...] = _dot(s, w_ref[...].astype(BF16)) + b_ref[...]


def _modulation(cond8, ada_w, ada_b):
    depth, d, n = ada_w.shape
    tn = min(n, 1024)
    return pl.pallas_call(
        _mod_kernel,
        grid=(depth, n // tn),
        in_specs=[
            pl.BlockSpec((SUBLANES, d), lambda l, j: (0, 0)),
            pl.BlockSpec((None, d, tn), lambda l, j: (l, 0, j)),
            pl.BlockSpec((None, 1, tn), lambda l, j: (l, 0, j)),
        ],
        out_specs=pl.BlockSpec((None, SUBLANES, tn), lambda l, j: (l, 0, j)),
        out_shape=jax.ShapeDtypeStruct((depth, SUBLANES, n), F32),
        compiler_params=_params("arbitrary", "arbitrary", vmem_bytes=d * tn * (2 * 4 + 2) + 2 * MIB),
        name="modulation",
    )(cond8, ada_w, ada_b.reshape(depth, 1, n))


def _on_tile_part(parts, tile, n_ctx_tiles, fn):
    if len(parts) == 1:
        fn(parts[0], tile)
    else:
        pl.when(tile < n_ctx_tiles)(lambda: fn(parts[0], tile))
        pl.when(tile >= n_ctx_tiles)(lambda: fn(parts[1], tile - n_ctx_tiles))


def _ffn_kernel(*refs, n_x, n_out, n_ctx_tiles, k, res_w):
    x_hbm = refs[:n_x]
    mod_ref, gpre_ref, gpost_ref, wg_ref, wu_ref, wd_ref = refs[n_x:n_x + 6]
    o_hbm = refs[n_x + 6:n_x + 6 + n_out]
    h_ref, x_buf, acc_ref, out_sem, x_sem = refs[n_x + 6 + n_out:]
    i = pl.program_id(0)
    j = pl.program_id(1)
    tm = x_buf.shape[0]
    n_blk = x_sem.shape[0]
    blk = tm // n_blk

    def out_copy(ref, tile):
        rows = pl.ds(pl.multiple_of(tile * tm, tm), tm)
        return pltpu.make_async_copy(acc_ref, ref.at[rows, :], out_sem)

    def x_copy(ref, tile, b):
        src = pl.ds(pl.multiple_of(tile * tm + b * blk, blk), blk)
        dst = pl.ds(pl.multiple_of(b * blk, blk), blk)
        return pltpu.make_async_copy(ref.at[src, :], x_buf.at[dst, :], x_sem.at[b])

    @pl.when(j == 0)
    def _():
        def fetch_and_prologue(ref, t):
            for b in range(n_blk):
                x_copy(ref, t, b).start()

            def body(b, carry):
                x_copy(ref, t, b).wait()
                rows = pl.ds(pl.multiple_of(b * blk, blk), blk)
                h_ref[rows, :] = _pre_mod(x_buf[rows, :], gpre_ref[...], mod_ref, k).astype(BF16)
                return carry

            lax.fori_loop(0, n_blk, body, 0)

        _on_tile_part(x_hbm, i, n_ctx_tiles, fetch_and_prologue)

        @pl.when(i > 0)
        def _():
            _on_tile_part(o_hbm, i - 1, n_ctx_tiles, lambda ref, t: out_copy(ref, t).wait())

        acc_ref[...] = jnp.zeros_like(acc_ref)

    h = h_ref[...]
    half = wg_ref.shape[1] // 2
    for c0 in (0, half):
        g = _dot(h, wg_ref[:, c0:c0 + half].astype(BF16))
        u = _dot(h, wu_ref[:, c0:c0 + half].astype(BF16))
        a = (g * jax.nn.sigmoid(g) * u).astype(BF16)
        acc_ref[...] += _dot(a, wd_ref[c0:c0 + half, :].astype(BF16))

    @pl.when(j == pl.num_programs(1) - 1)
    def _():
        def epilogue(rows):
            acc_ref[rows, :] = _post_add(x_buf[rows, :], acc_ref[rows, :], gpost_ref[...], mod_ref, k, res_w)

        _for_row_blocks(tm, epilogue)
        _on_tile_part(o_hbm, i, n_ctx_tiles, lambda ref, t: out_copy(ref, t).start())

        @pl.when(i == pl.num_programs(0) - 1)
        def _():
            _on_tile_part(o_hbm, i, n_ctx_tiles, lambda ref, t: out_copy(ref, t).wait())


def _ffn(xs, mod, gpre, gpost, wgu, wd, *, layer, slot, k, tm, tf, split_out, group_rows, n_ctx_groups):
    m = sum(x.shape[0] for x in xs)
    d = xs[0].shape[1]
    nf = wd.shape[2] // tf
    n_ctx_tiles = n_ctx_groups * group_rows // tm
    out_rows = (n_ctx_tiles * tm, m - n_ctx_tiles * tm) if split_out else (m,)
    midx = functools.partial(_mod_index, tm=tm, group_rows=group_rows, n_ctx_groups=n_ctx_groups)
    hbm = pl.BlockSpec(memory_space=pl.ANY)
    return pl.pallas_call(
        functools.partial(_ffn_kernel, n_x=len(xs), n_out=len(out_rows), n_ctx_tiles=n_ctx_tiles, k=k,
                          res_w=0.5),
        grid=(m // tm, nf),
        in_specs=[hbm] * len(xs) + [
            pl.BlockSpec((None, N_MOD, d), lambda i, j: (midx(i), 0, 0)),
            pl.BlockSpec((1, d), lambda i, j: (0, 0)),
            pl.BlockSpec((1, d), lambda i, j: (0, 0)),
            pl.BlockSpec((None, None, d, tf), lambda i, j: (layer, slot, 0, j)),
            pl.BlockSpec((None, None, d, tf), lambda i, j: (layer, slot, 0, j + nf)),
            pl.BlockSpec((None, None, tf, d), lambda i, j: (layer, slot, j, 0)),
        ],
        out_specs=[hbm] * len(out_rows),
        out_shape=[jax.ShapeDtypeStruct((r, d), F32) for r in out_rows],
        scratch_shapes=[
            pltpu.VMEM((tm, d), BF16),
            pltpu.VMEM((tm, d), F32),
            pltpu.VMEM((tm, d), F32),
            pltpu.SemaphoreType.DMA(()),
            pltpu.SemaphoreType.DMA((tm // min(ROW_BLOCK, tm),)),
        ],
        compiler_params=_params("arbitrary", "arbitrary",
                                vmem_bytes=tm * d * (2 + 4 + 4) + 2 * 3 * d * tf * 4
                                + (tf // 2) * (tm * (4 + 4 + 2) + 3 * d * 2) + 2 * MIB),
        name="ffn",
    )(*xs, mod, gpre, gpost, wgu, wgu, wd)


def _dense_post_kernel(*refs, n_a, n_ctx_tiles, k, res_w):
    a_refs = refs[:n_a]
    w_ref, x_ref, mod_ref, gpost_ref, o_ref = refs[n_a:]

    def body(a_ref):
        o_ref[...] = _dot(a_ref[...], w_ref[...])

        def block(rows):
            o_ref[rows, :] = _post_add(x_ref[rows, :], o_ref[rows, :], gpost_ref[...], mod_ref, k, res_w)

        _for_row_blocks(o_ref.shape[0], block)

    _on_tile_ref(a_refs, pl.program_id(0), n_ctx_tiles, body)


def _dense_post(a_parts, w, x, mod, gpost, *, widx, k, tm, group_rows, n_ctx_groups):
    m, d = x.shape
    kk = a_parts[0].shape[1]
    n_ctx_tiles = n_ctx_groups * group_rows // tm
    midx = functools.partial(_mod_index, tm=tm, group_rows=group_rows, n_ctx_groups=n_ctx_groups)
    return pl.pallas_call(
        functools.partial(_dense_post_kernel, n_a=len(a_parts), n_ctx_tiles=n_ctx_tiles, k=k, res_w=1.0),
        grid=(m // tm,),
        in_specs=_split_row_specs(a_parts, tm, kk, n_ctx_tiles, 1) + [
            pl.BlockSpec((None, kk, d), lambda i: (widx, 0, 0)),
            pl.BlockSpec((tm, d), lambda i: (i, 0)),
            pl.BlockSpec((None, N_MOD, d), lambda i: (midx(i), 0, 0)),
            pl.BlockSpec((1, d), lambda i: (0, 0)),
        ],
        out_specs=pl.BlockSpec((tm, d), lambda i: (i, 0)),
        out_shape=jax.ShapeDtypeStruct((m, d), F32),
        compiler_params=_params("arbitrary", vmem_bytes=2 * (len(a_parts) * tm * kk * 2 + kk * d * 2
                                                             + 2 * tm * d * 4) + 2 * tm * d * 4),
        name="dense_post",
    )(*a_parts, w, x, mod, gpost)


def _dft_tables(t, w):
    def cs(n):
        kn = np.outer(np.arange(n), np.arange(n)) % n
        ang = 2.0 * np.pi * kn.astype(np.float64) / n
        return np.cos(ang) / np.sqrt(n), np.sin(ang) / np.sqrt(n)

    ct, st = cs(t)
    cw, sw = cs(w)
    as_bf16 = lambda a: jnp.asarray(a, dtype=F32).astype(BF16)
    return as_bf16(np.concatenate([ct, st], axis=0)), as_bf16(cw), as_bf16(-sw)


def _fourier_kernel(x_ref, mod_ref, gpre_ref, cst_ref, cw_ref, swn_ref, o_ref, *, k):
    t = x_ref.shape[0]
    w = cw_ref.shape[0]
    h = _pre_mod(x_ref[...], gpre_ref[...], mod_ref, k).astype(BF16)
    cst = cst_ref[...]
    for g in range(FOURIER_GROUPS):
        p = _dot(cst, h[:, g * w:(g + 1) * w]).astype(BF16)
        f = _dot(p[:t], cw_ref[...]) + _dot(p[t:], swn_ref[...])
        o_ref[:, g * w:(g + 1) * w] = f.astype(BF16)


def _fourier(x, mod, gpre, *, t, nb, row_block0, mod0, k):
    d = x.shape[1]
    w = d // FOURIER_GROUPS
    cst, cw, swn = _dft_tables(t, w)
    return pl.pallas_call(
        functools.partial(_fourier_kernel, k=k),
        grid=(nb,),
        in_specs=[
            pl.BlockSpec((t, d), lambda b: (row_block0 + b, 0)),
            pl.BlockSpec((None, N_MOD, d), lambda b: (mod0(b), 0, 0)),
            pl.BlockSpec((1, d), lambda b: (0, 0)),
            pl.BlockSpec((2 * t, t), lambda b: (0, 0)),
            pl.BlockSpec((w, w), lambda b: (0, 0)),
            pl.BlockSpec((w, w), lambda b: (0, 0)),
        ],
        out_specs=pl.BlockSpec((t, d), lambda b: (b, 0)),
        out_shape=jax.ShapeDtypeStruct((nb * t, d), BF16),
        compiler_params=_params("arbitrary", vmem_bytes=2 * (t * d * 4 + 2 * t * t * 2 + 2 * w * w * 2
                                                             + t * d * 2) + t * d * 2 + 2 * t * w * 8),
        name="fourier",
    )(x, mod, gpre, cst, cw, swn)


def _rope_tables(t, head_k):
    rows = t // GRID_W
    pairs = head_k // 4
    row = np.repeat(np.arange(rows), GRID_W).astype(np.float64)
    col = np.tile(np.arange(GRID_W), rows).astype(np.float64)
    inv = ROPE_BASE ** (-np.arange(pairs, dtype=np.float64) / pairs)
    ang = np.concatenate([row[:, None] * inv, col[:, None] * inv], axis=-1)
    cos, sin = np.cos(ang), np.sin(ang)
    ce = np.repeat(cos, 2, axis=-1)
    se = np.stack([-sin, sin], axis=-1).reshape(t, head_k)
    return jnp.asarray(ce, dtype=F32), jnp.asarray(se, dtype=F32)


def _gla_in_kernel(x_ref, mod_ref, gpre_ref, w_ref, wlr_ref, wz_ref, bz_ref, cs_ref, ce_ref, se_ref,
                   p_ref, g_ref, h_ref, lr_ref, *, k, n_ctx_tiles, n_qk_steps, head_k):
    i = pl.program_id(0)
    j = pl.program_id(1)

    @pl.when(j == 0)
    def _():
        def prologue(rows):
            h = _pre_mod(x_ref[rows, :], gpre_ref[...], mod_ref, k).astype(BF16)
            h_ref[rows, :] = h
            lr_ref[rows, :] = _dot_nt(h, wlr_ref[...].astype(BF16)).astype(BF16)

        _for_row_blocks(x_ref.shape[0], prologue)

    def step(rope):
        z = _dot(lr_ref[...], wz_ref[...]) + bz_ref[...]
        g_ref[...] = (jnp.minimum(z, 0.0) - jnp.log1p(jnp.exp(-jnp.abs(z)))) * (LOG2_E / GATE_TAU)
        h = h_ref[...]
        for s in range(w_ref.shape[0] // head_k):
            cols = slice(s * head_k, (s + 1) * head_k)
            acc = _dot_nt(h, w_ref[cols, :].astype(BF16)) * cs_ref[:, cols]
            if rope:
                lane = lax.broadcasted_iota(jnp.int32, acc.shape, 1)
                partner = jnp.where(lane % 2 == 0, pltpu.roll(acc, head_k - 1, 1), pltpu.roll(acc, 1, 1))
                acc = acc * ce_ref[...] + partner * se_ref[...]
            p_ref[:, cols] = acc.astype(BF16)

    rope = jnp.logical_and(i >= n_ctx_tiles, j < n_qk_steps)
    pl.when(rope)(lambda: step(True))
    pl.when(jnp.logical_not(rope))(lambda: step(False))


def _gla_in(x, mod, gpre, w_in_t, wz, bz, colscale, *, widx, n_main, k, tm, n_steps, head_k, dk_tot,
            group_rows, n_ctx_groups):
    m, d = x.shape
    n = n_main
    tn = n // n_steps
    gcols = 2 * dk_tot // n_steps
    n_lr = w_in_t.shape[1] - n_main
    assert (4 * dk_tot) % tn == 0 and tn % head_k == 0 and gcols % LANES == 0 and n_main % n_lr == 0
    n_ctx_tiles = n_ctx_groups * group_rows // tm
    tiles_per_group = group_rows // tm
    ce, se = _rope_tables(group_rows, head_k)
    midx = functools.partial(_mod_index, tm=tm, group_rows=group_rows, n_ctx_groups=n_ctx_groups)

    def tab_idx(i, j):
        return (jnp.where(i >= n_ctx_tiles, (i - n_ctx_tiles) % tiles_per_group, 0), 0)

    return pl.pallas_call(
        functools.partial(_gla_in_kernel, k=k, n_ctx_tiles=n_ctx_tiles, n_qk_steps=4 * dk_tot // tn,
                          head_k=head_k),
        grid=(m // tm, n_steps),
        in_specs=[
            pl.BlockSpec((tm, d), lambda i, j: (i, 0)),
            pl.BlockSpec((None, N_MOD, d), lambda i, j: (midx(i), 0, 0)),
            pl.BlockSpec((1, d), lambda i, j: (0, 0)),
            pl.BlockSpec((None, tn, d), lambda i, j: (widx, j, 0)),
            pl.BlockSpec((None, n_lr, d), lambda i, j: (widx, n_main // n_lr, 0)),
            pl.BlockSpec((wz.shape[0], gcols), lambda i, j: (0, j)),
            pl.BlockSpec((1, gcols), lambda i, j: (0, j)),
            pl.BlockSpec((1, tn), lambda i, j: (0, j)),
            pl.BlockSpec((tm, head_k), tab_idx),
            pl.BlockSpec((tm, head_k), tab_idx),
        ],
        out_specs=[
            pl.BlockSpec((tm, tn), lambda i, j: (i, j)),
            pl.BlockSpec((tm, gcols), lambda i, j: (i, j)),
        ],
        out_shape=[
            jax.ShapeDtypeStruct((m, n), BF16),
            jax.ShapeDtypeStruct((m, 2 * dk_tot), F32),
        ],
        scratch_shapes=[pltpu.VMEM((tm, d), BF16), pltpu.VMEM((tm, n_lr), BF16)],
        compiler_params=_params("arbitrary", "arbitrary",
                                vmem_bytes=2 * (tm * d * 4 + d * tn * 4 + tm * tn * 2 + tm * gcols * 4
                                                + 2 * tm * head_k * 4) + tm * d * 2
                                + 2 * head_k * (tm * 4 + d * 2) + 2 * tm * gcols * 4 + 2 * MIB),
        name="gla_in",
    )(x, mod, gpre, w_in_t, w_in_t, wz, bz, colscale, ce, se)


def _scan_constants():
    c = CHUNK
    idx = np.arange(c)
    tri = (idx[None, :] <= idx[:, None]).astype(np.float64)
    masks = [np.eye(c)]
    for lev in range(N_LEVELS):
        h = 1 << lev
        start = idx // (2 * h) * (2 * h)
        upper = (idx - start) >= h
        same = start[:, None] == start[None, :]
        masks.append((same & upper[:, None] & ~upper[None, :]).astype(np.float64))
    m_f = np.stack(masks, axis=0)
    m_b = m_f[:, ::-1, ::-1]
    t_f = np.concatenate([tri, tri], axis=1)
    t_b = np.concatenate([tri[::-1, ::-1]] * 2, axis=1)
    return (jnp.asarray(t_f, dtype=BF16), jnp.asarray(t_b, dtype=BF16),
            jnp.asarray(m_f, dtype=F32), jnp.asarray(m_b, dtype=F32))


def _level_exponents(b, g, fwd):
    c, kk = b.shape
    tile = (c // SUBLANES, SUBLANES, kk)
    b8 = b.reshape(tile)
    r8 = lax.broadcasted_iota(jnp.int32, tile, 1)
    out = [jnp.where(r8 % 2 == (1 if fwd else 0), g.reshape(tile), 0.0).reshape(c, kk)]
    lo, hi = (1, 5) if fwd else (2, 6)
    mids = [jnp.where(r8 < 4, b8[:, lo:lo + 1, :], b8[:, hi:hi + 1, :]), b8[:, 3:4, :] if fwd else b8[:, 4:5, :]]
    for lev, mid in zip((1, 2), mids):
        h = 1 << lev
        second_half = (r8 % (2 * h)) >= h
        sign = jnp.where(second_half if fwd else jnp.logical_not(second_half), 1.0, -1.0)
        out.append(((b8 - mid) * sign).reshape(c, kk))
    for lev in range(3, N_LEVELS):
        h = 1 << lev
        bb = b.reshape(c // (2 * h), 2 * h, kk)
        first, second = bb[:, :h, :], bb[:, h:, :]
        if fwd:
            mid = bb[:, h - 1:h, :]
            t = jnp.concatenate([mid - first, second - mid], axis=1)
        else:
            mid = bb[:, h:h + 1, :]
            t = jnp.concatenate([first - mid, mid - second], axis=1)
        out.append(t.reshape(c, kk))
    return out


def _scan_chunks(chains):
    c = CHUNK

    st1 = []
    for q_ref, k_ref, g_ref, v_ref, tri_ref, m_ref, s_ref, o_ref, rows, kcols, vcols, fwd in chains:
        g = g_ref[rows, kcols]
        g_hi = g.astype(BF16)
        g_lo = (g - g_hi.astype(F32)).astype(BF16)
        b = _dot(tri_ref[...], jnp.concatenate([g_hi, g_lo], axis=0))
        diag = _dot_nt(q_ref[rows, kcols], k_ref[rows, kcols])
        st1.append((g, b, diag))

    st2 = []
    for (q_ref, k_ref, g_ref, v_ref, tri_ref, m_ref, s_ref, o_ref, rows, kcols, vcols, fwd), (g, b, diag) in zip(
            chains, st1):
        q_bf = q_ref[rows, kcols]
        k_bf = k_ref[rows, kcols]
        scores = m_ref[0] * diag
        for lev, t in enumerate(_level_exponents(b, g, fwd)):
            e = jnp.exp2(t).astype(BF16)
            scores += m_ref[lev + 1] * _dot_nt(q_bf * e, k_bf * e)
        st2.append(scores)

    for (q_ref, k_ref, g_ref, v_ref, tri_ref, m_ref, s_ref, o_ref, rows, kcols, vcols, fwd), (g, b, diag), scores in zip(
            chains, st1, st2):
        qe = q_ref[rows, kcols] * jnp.exp2(b).astype(BF16)
        o_ref[rows, vcols] = _dot(qe, s_ref[...].astype(BF16)) + _dot(scores.astype(BF16), v_ref[rows, vcols])

    for (q_ref, k_ref, g_ref, v_ref, tri_ref, m_ref, s_ref, o_ref, rows, kcols, vcols, fwd), (g, b, diag) in zip(
            chains, st1):
        last = c - 1 if fwd else 0
        b_last = b[last:last + 1]
        kt = k_ref[rows, kcols] * jnp.exp2(b_last - b).astype(BF16)
        e_col = jnp.broadcast_to(jnp.exp2(b_last), (LANES, b.shape[1])).T
        s = s_ref[...]
        decay = jnp.concatenate([e_col] * (s.shape[1] // LANES), axis=1)
        s_ref[...] = decay * s + _dot_tn(kt, v_ref[rows, vcols])


def _gla_scan_kernel(*refs, has_s0, has_sout, head_k, head_v):
    qf_ref, kf_ref, qb_ref, kb_ref, v_ref, r_ref, gf_ref, gb_ref = refs[:8]
    pos = 8
    if has_s0:
        s0_ref = refs[pos]
        pos += 1
    tf_ref, tb_ref, mf_ref, mb_ref, gn_ref = refs[pos:pos + 5]
    pos += 5
    o_ref = refs[pos]
    pos += 1
    if has_sout:
        sout_ref = refs[pos]
        pos += 1
    s_ref, of_ref, ob_ref = refs[pos:pos + 3]

    t = v_ref.shape[0]
    n = t // CHUNK
    hps = v_ref.shape[1] // head_v
    if has_s0:
        s_ref[...] = s0_ref[...]
    else:
        s_ref[...] = jnp.zeros_like(s_ref)

    def body(ci, carry):
        rows_f = pl.ds(pl.multiple_of(ci * CHUNK, CHUNK), CHUNK)
        rows_b = pl.ds(pl.multiple_of((n - 1 - ci) * CHUNK, CHUNK), CHUNK)
        chains = []
        for hh in range(hps):
            kcols = slice(hh * head_k, (hh + 1) * head_k)
            vcols = slice(hh * head_v, (hh + 1) * head_v)
            chains.append((qf_ref, kf_ref, gf_ref, v_ref, tf_ref, mf_ref, s_ref.at[0, hh], of_ref, rows_f,
                           kcols, vcols, True))
            chains.append((qb_ref, kb_ref, gb_ref, v_ref, tb_ref, mb_ref, s_ref.at[1, hh], ob_ref, rows_b,
                           kcols, vcols, False))
        _scan_chunks(chains)
        return carry

    lax.fori_loop(0, n, body, 0)

    if has_sout:
        sout_ref[...] = s_ref[...]

    def norm_body(ci, carry):
        rows = pl.ds(pl.multiple_of(ci * CHUNK, CHUNK), CHUNK)
        for hh in range(hps):
            vcols = slice(hh * head_v, (hh + 1) * head_v)
            o = of_ref[rows, vcols] + ob_ref[rows, vcols]
            r = r_ref[rows, vcols].astype(F32)
            o_ref[rows, vcols] = (_rms(o) * gn_ref[...] * (r * jax.nn.sigmoid(r))).astype(BF16)
        return carry

    lax.fori_loop(0, n, norm_body, 0)


def _gla_scan(p, g, gnorm, s0, *, t, nb, row_block0, heads, hps, head_k, head_v, want_state):
    assert heads % hps == 0
    nhb = heads // hps
    t_f, t_b, m_f, m_b = _scan_constants()
    qk_blk = lambda off: pl.BlockSpec((t, hps * head_k), lambda b, h: (row_block0 + b, off * nhb + h))
    v_off = 4 * heads * head_k // (hps * head_v)
    state_blk = pl.BlockSpec((None, None, 2, hps, head_k, head_v), lambda b, h: (b, 0, 0, h, 0, 0))
    in_specs = [
        qk_blk(0), qk_blk(1), qk_blk(2), qk_blk(3),
        pl.BlockSpec((t, hps * head_v), lambda b, h: (row_block0 + b, v_off + h)),
        pl.BlockSpec((t, hps * head_v), lambda b, h: (row_block0 + b, v_off + nhb + h)),
        pl.BlockSpec((t, hps * head_k), lambda b, h: (row_block0 + b, h)),
        pl.BlockSpec((t, hps * head_k), lambda b, h: (row_block0 + b, nhb + h)),
    ]
    args = [p, p, p, p, p, p, g, g]
    if s0 is not None:
        in_specs.append(state_blk)
        args.append(s0)
    in_specs += [
        pl.BlockSpec(t_f.shape, lambda b, h: (0, 0)),
        pl.BlockSpec(t_b.shape, lambda b, h: (0, 0)),
        pl.BlockSpec(m_f.shape, lambda b, h: (0, 0, 0)),
        pl.BlockSpec(m_b.shape, lambda b, h: (0, 0, 0)),
        pl.BlockSpec((1, head_v), lambda b, h: (0, 0)),
    ]
    args += [t_f, t_b, m_f, m_b, gnorm]
    out_specs = [pl.BlockSpec((t, hps * head_v), lambda b, h: (b, h))]
    out_shape = [jax.ShapeDtypeStruct((nb * t, heads * head_v), BF16)]
    if want_state:
        out_specs.append(state_blk)
        out_shape.append(jax.ShapeDtypeStruct((nb, 1, 2, heads, head_k, head_v), F32))
    return pl.pallas_call(
        functools.partial(_gla_scan_kernel, has_s0=s0 is not None, has_sout=want_state, head_k=head_k,
                          head_v=head_v),
        grid=(nb, nhb),
        in_specs=in_specs,
        out_specs=out_specs,
        out_shape=out_shape,
        scratch_shapes=[
            pltpu.VMEM((2, hps, head_k, head_v), F32),
            pltpu.VMEM((t, hps * head_v), F32),
            pltpu.VMEM((t, hps * head_v), F32),
        ],
        compiler_params=_params(
            "arbitrary", "arbitrary",
            vmem_bytes=2 * t * hps * (4 * head_k * 2 + 3 * head_v * 2 + 2 * head_k * 4)
            + (2 * (int(s0 is not None) + int(want_state)) + 1) * 2 * hps * head_k * head_v * 4
            + 2 * t * hps * head_v * 4 + 8 * MIB),
        name="gla_scan",
    )(*args)


def kernel(x_prompt, x_sample, state_gla, c, c_ctx, ada_w, ada_b, norm_pre, norm_post, ffn_w_gate_up,
           ffn_w_down, fourier_w, gla_w_in, gla_w_gate_up, gla_b_gate, gla_norm, gla_w_out):
    batch, seq, d = x_prompt.shape
    dec_batch, dec_seq, _ = x_sample.shape
    depth = ada_w.shape[0]
    heads, head_k, head_v = state_gla.shape[3:]
    dk_tot = heads * head_k
    dv_tot = heads * head_v
    rank = gla_w_gate_up.shape[2]
    d_ff = ffn_w_down.shape[2]
    ctx_rows = batch * seq
    assert ctx_rows % dec_seq == 0 and dec_seq % seq == 0 and 1 + dec_batch <= SUBLANES
    n_ctx_groups = ctx_rows // dec_seq
    grp = dict(group_rows=dec_seq, n_ctx_groups=n_ctx_groups)
    tm_small = min(dec_seq, 512)
    tm_big = min(dec_seq, 1024)
    tf = 512 if d_ff % 512 == 0 else d_ff

    cond8 = jnp.zeros((SUBLANES, d), F32).at[0].set(c_ctx).at[1:1 + dec_batch].set(c)
    mods = _modulation(cond8, ada_w, ada_b)[:, :1 + dec_batch].reshape(depth, 1 + dec_batch, N_MOD, d)

    fw_bf = fourier_w.astype(BF16)
    n_main = 4 * dk_tot + 2 * dv_tot
    w_in_t = jnp.swapaxes(gla_w_in, 1, 2)
    wout_bf = gla_w_out.astype(BF16)

    xs = (x_prompt.reshape(ctx_rows, d), x_sample.reshape(dec_batch * dec_seq, d))
    states = []
    for l in range(depth):
        mod = mods[l]
        j = l // 2
        npre = lambda s: norm_pre[l, s][None, :]
        npost = lambda s: norm_post[l, s][None, :]
        ffn = functools.partial(_ffn, mod=mod, wgu=ffn_w_gate_up, wd=ffn_w_down, layer=l, tf=tf, tm=tm_big,
                                **grp)
        (x,) = ffn(xs, gpre=npre(0), gpost=npost(0), slot=0, k=0, split_out=False)
        if l % 2 == 0:
            f_ctx = _fourier(x, mod, npre(1), t=seq, nb=batch, row_block0=0, mod0=lambda b: 0, k=1)
            f_lat = _fourier(x, mod, npre(1), t=dec_seq, nb=dec_batch, row_block0=n_ctx_groups,
                             mod0=lambda b: 1 + b, k=1)
            x = _dense_post((f_ctx, f_lat), fw_bf, x, mod, npost(1), widx=j, k=1, tm=tm_small, **grp)
        else:
            zero = jnp.zeros((rank, dk_tot), F32)
            wz = jnp.concatenate([jnp.concatenate([gla_w_gate_up[j, 0], zero], axis=1),
                                  jnp.concatenate([zero, gla_w_gate_up[j, 1]], axis=1)], axis=0).astype(BF16)
            bz = gla_b_gate[j].reshape(1, 2 * dk_tot)
            colscale = jnp.ones((n_main,), F32).at[:dk_tot].set(head_k ** -0.5)
            colscale = colscale.at[2 * dk_tot:3 * dk_tot].set(head_k ** -0.5).reshape(1, n_main)
            p, g = _gla_in(x, mod, npre(1), w_in_t, wz, bz, colscale, widx=j, n_main=n_main, k=1,
                           tm=tm_big, n_steps=8, head_k=head_k, dk_tot=dk_tot, **grp)
            gn = gla_norm[j][None, :]
            scan = functools.partial(_gla_scan, p, g, gn, heads=heads, head_k=head_k, head_v=head_v)
            o_ctx, st = scan(None, t=seq, nb=batch, row_block0=0, hps=heads, want_state=True)
            (o_lat,) = scan(state_gla[:, j:j + 1], t=dec_seq, nb=dec_batch, row_block0=n_ctx_groups,
                            hps=min(heads, max(1, 2 * heads * seq // dec_seq)), want_state=False)
            states.append(st)
            x = _dense_post((o_ctx, o_lat), wout_bf, x, mod, npost(1), widx=j, k=1, tm=tm_small, **grp)
        xs = tuple(ffn((x,), gpre=npre(2), gpost=npost(2), slot=1, k=2, split_out=l == depth - 1))

    y_prompt = xs[0].reshape(batch, seq, d)
    y_sample = xs[1].reshape(dec_batch, dec_seq, d)
    new_state = jnp.concatenate(states, axis=1)
    return (y_prompt, y_sample, new_state)
```

```python
import functools

import numpy as np
import jax
import jax.numpy as jnp
from jax import lax
from jax.experimental import pallas as pl
from jax.experimental.pallas import tpu as pltpu

F32 = jnp.float32
BF16 = jnp.bfloat16

EPS = 1e-6
N_MOD = 9
GRID_W = 64
FOURIER_GROUPS = 4
GATE_TAU = 16.0
CHUNK = 64
ROPE_BASE = 10000.0
LOG2_E = 1.4426950408889634
N_LEVELS = 6
assert CHUNK == 1 << N_LEVELS

V7X_VMEM_BYTES = 64 * 1024 * 1024
MIB = 1024 * 1024
LANES = 128
SUBLANES = 8


def _params(*sem, vmem_bytes):
    request = (int(vmem_bytes) // MIB + 2) * MIB
    assert request < V7X_VMEM_BYTES, request
    return pltpu.CompilerParams(dimension_semantics=sem, vmem_limit_bytes=request)


def _dot(a, b):
    return jnp.dot(a, b, preferred_element_type=F32)


def _dot_nt(a, b):
    return lax.dot_general(a, b, (((1,), (1,)), ((), ())), preferred_element_type=F32)


def _dot_tn(a, b):
    return lax.dot_general(a, b, (((0,), (0,)), ((), ())), preferred_element_type=F32)


def _rms(x):
    return x * lax.rsqrt(jnp.mean(x * x, axis=-1, keepdims=True) + EPS)


def _pre_mod(x, gpre, mod_ref, k):
    gs = gpre * (1.0 + mod_ref[3 * k + 1:3 * k + 2, :])
    return _rms(x) * gs + mod_ref[3 * k:3 * k + 1, :]


def _post_add(x, out, gpost, mod_ref, k, res_w):
    gg = (res_w * mod_ref[3 * k + 2:3 * k + 3, :]) * gpost
    return x + _rms(out) * gg


ROW_BLOCK = 128


def _for_row_blocks(n_rows, fn):
    block = min(ROW_BLOCK, n_rows)

    def body(r, carry):
        fn(pl.ds(pl.multiple_of(r * block, block), block))
        return carry

    lax.fori_loop(0, n_rows // block, body, 0)


def _mod_index(i, tm, group_rows, n_ctx_groups):
    return jnp.maximum(i * tm // group_rows - (n_ctx_groups - 1), 0)


def _on_tile_ref(refs, i, n_ctx_tiles, fn):
    if len(refs) == 1:
        fn(refs[0])
    else:
        pl.when(i < n_ctx_tiles)(lambda: fn(refs[0]))
        pl.when(i >= n_ctx_tiles)(lambda: fn(refs[1]))


def _split_row_specs(arrays, tm, width, n_ctx_tiles, ngrid, buffers=None):
    kw = {} if buffers is None else dict(pipeline_mode=pl.Buffered(buffers))
    if ngrid == 1:
        wrap = lambda f: (lambda i: f(i))
    else:
        wrap = lambda f: (lambda i, j: f(i))
    if len(arrays) == 1:
        return [pl.BlockSpec((tm, width), wrap(lambda i: (i, 0)), **kw)]
    return [
        pl.BlockSpec((tm, width), wrap(lambda i: (jnp.minimum(i, n_ctx_tiles - 1), 0)), **kw),
        pl.BlockSpec((tm, width), wrap(lambda i: (jnp.maximum(i - n_ctx_tiles, 0), 0)), **kw),
    ]


def _mod_kernel(c_ref, w_ref, b_ref, o_ref):
    c = c_ref[...]
    s = (c * jax.nn.sigmoid(c)).astype(BF16)
    o_ref[...] = _dot(s, w_ref[...].astype(BF16)) + b_ref[...]


def _modulation(cond8, ada_w, ada_b):
    depth, d, n = ada_w.shape
    tn = min(n, 1024)
    return pl.pallas_call(
        _mod_kernel,
        grid=(depth, n // tn),
        in_specs=[
            pl.BlockSpec((SUBLANES, d), lambda l, j: (0, 0)),
            pl.BlockSpec((None, d, tn), lambda l, j: (l, 0, j)),
            pl.BlockSpec((None, 1, tn), lambda l, j: (l, 0, j)),
        ],
        out_specs=pl.BlockSpec((None, SUBLANES, tn), lambda l, j: (l, 0, j)),
        out_shape=jax.ShapeDtypeStruct((depth, SUBLANES, n), F32),
        compiler_params=_params("arbitrary", "arbitrary", vmem_bytes=d * tn * (2 * 4 + 2) + 2 * MIB),
        name="modulation",
    )(cond8, ada_w, ada_b.reshape(depth, 1, n))


def _on_tile_part(parts, tile, n_ctx_tiles, fn):
    if len(parts) == 1:
        fn(parts[0], tile)
    else:
        pl.when(tile < n_ctx_tiles)(lambda: fn(parts[0], tile))
        pl.when(tile >= n_ctx_tiles)(lambda: fn(parts[1], tile - n_ctx_tiles))


FFN_SUB_ROWS = 256


def _ffn_kernel(*refs, n_x, n_out, n_ctx_tiles, nf, k, res_w):
    x_hbm = refs[:n_x]
    mod_ref, gpre_ref, gpost_ref, wg_ref, wu_ref, wd_ref = refs[n_x:n_x + 6]
    o_hbm = refs[n_x + 6:n_x + 6 + n_out]
    h_ref, x_buf, acc_ref, out_sem, x_sem = refs[n_x + 6 + n_out:]
    i = pl.program_id(0)
    j = pl.program_id(1)
    tm = x_buf.shape[0]
    n_blk = x_sem.shape[0]
    blk = tm // n_blk

    def out_copy(ref, tile):
        rows = pl.ds(pl.multiple_of(tile * tm, tm), tm)
        return pltpu.make_async_copy(acc_ref, ref.at[rows, :], out_sem)

    def x_copy(ref, tile, b):
        src = pl.ds(pl.multiple_of(tile * tm + b * blk, blk), blk)
        dst = pl.ds(pl.multiple_of(b * blk, blk), blk)
        return pltpu.make_async_copy(ref.at[src, :], x_buf.at[dst, :], x_sem.at[b])

    n_sub = max(1, tm // FFN_SUB_ROWS)
    sub = tm // n_sub
    blk_per_sub = n_blk // n_sub
    last_j = pl.num_programs(1) - 1

    def swiglu(h, wg, wu, wd):
        g = _dot(h, wg)
        u = _dot(h, wu)
        return _dot((g * jax.nn.sigmoid(g) * u).astype(BF16), wd)

    half = wg_ref.shape[1] // 2

    def half_weights(c0):
        return (wg_ref[:, c0:c0 + half].astype(BF16), wu_ref[:, c0:c0 + half].astype(BF16),
                wd_ref[c0:c0 + half, :].astype(BF16))

    def edge_pass(weights, do_pro, assign, do_epi, x_src):
        rows = [slice(r * sub, (r + 1) * sub) for r in range(n_sub)]

        def prologue(r):
            h = _pre_mod(x_buf[rows[r], :], gpre_ref[...], mod_ref, k).astype(BF16)
            h_ref[rows[r], :] = h
            return h

        def wait_x(r0, r1):
            for b in range(r0 * blk_per_sub, r1 * blk_per_sub):
                x_copy(*x_src, b).wait()

        hs = {}
        if do_pro:
            for b in range(n_blk):
                x_copy(*x_src, b).start()
            for r in range(min(2, n_sub)):
                wait_x(r, r + 1)
                hs[r] = prologue(r)
        for r in range(n_sub):
            y = swiglu(hs.pop(r) if do_pro else h_ref[rows[r], :], *weights)
            if do_pro and r == 0:
                @pl.when(i > 0)
                def _():
                    _on_tile_part(o_hbm, i - 1, n_ctx_tiles, lambda ref, t: out_copy(ref, t).wait())

                wait_x(min(2, n_sub), n_sub)
            if do_pro and r + 2 < n_sub:
                hs[r + 2] = prologue(r + 2)
            if not assign:
                y = acc_ref[rows[r], :] + y
            if do_epi:
                y = _post_add(x_buf[rows[r], :], y, gpost_ref[...], mod_ref, k, res_w)
            acc_ref[rows[r], :] = y

    def edge_step(first, last, x_src):
        edge_pass(half_weights(0), first, first, False, x_src)
        edge_pass(half_weights(half), False, False, last, x_src)
        if last:
            _on_tile_part(o_hbm, i, n_ctx_tiles, lambda ref, t: out_copy(ref, t).start())

            @pl.when(i == pl.num_programs(0) - 1)
            def _():
                _on_tile_part(o_hbm, i, n_ctx_tiles, lambda ref, t: out_copy(ref, t).wait())

    def first_step(last):
        _on_tile_part(x_hbm, i, n_ctx_tiles, lambda ref, t: edge_step(True, last, (ref, t)))

    if nf == 1:
        first_step(True)
        return
    pl.when(j == 0)(lambda: first_step(False))
    pl.when(j == last_j)(lambda: edge_step(False, True, None))

    @pl.when(jnp.logical_and(j > 0, j < last_j))
    def _():
        h = h_ref[...]
        for c0 in (0, half):
            acc_ref[...] += swiglu(h, *half_weights(c0))


def _ffn(xs, mod, gpre, gpost, wgu, wd, *, layer, slot, k, tm, tf, split_out, group_rows, n_ctx_groups):
    m = sum(x.shape[0] for x in xs)
    d = xs[0].shape[1]
    nf = wd.shape[2] // tf
    n_ctx_tiles = n_ctx_groups * group_rows // tm
    out_rows = (n_ctx_tiles * tm, m - n_ctx_tiles * tm) if split_out else (m,)
    midx = functools.partial(_mod_index, tm=tm, group_rows=group_rows, n_ctx_groups=n_ctx_groups)
    hbm = pl.BlockSpec(memory_space=pl.ANY)
    return pl.pallas_call(
        functools.partial(_ffn_kernel, n_x=len(xs), n_out=len(out_rows), n_ctx_tiles=n_ctx_tiles, nf=nf, k=k,
                          res_w=0.5),
        grid=(m // tm, nf),
        in_specs=[hbm] * len(xs) + [
            pl.BlockSpec((None, N_MOD, d), lambda i, j: (midx(i), 0, 0)),
            pl.BlockSpec((1, d), lambda i, j: (0, 0)),
            pl.BlockSpec((1, d), lambda i, j: (0, 0)),
            pl.BlockSpec((None, None, d, tf), lambda i, j: (layer, slot, 0, j)),
            pl.BlockSpec((None, None, d, tf), lambda i, j: (layer, slot, 0, j + nf)),
            pl.BlockSpec((None, None, tf, d), lambda i, j: (layer, slot, j, 0)),
        ],
        out_specs=[hbm] * len(out_rows),
        out_shape=[jax.ShapeDtypeStruct((r, d), F32) for r in out_rows],
        scratch_shapes=[
            pltpu.VMEM((tm, d), BF16),
            pltpu.VMEM((tm, d), F32),
            pltpu.VMEM((tm, d), F32),
            pltpu.SemaphoreType.DMA(()),
            pltpu.SemaphoreType.DMA((tm // min(ROW_BLOCK, tm),)),
        ],
        compiler_params=_params("arbitrary", "arbitrary",
                                vmem_bytes=tm * d * (2 + 4 + 4) + 2 * 3 * d * tf * 4
                                + (tf // 2) * (tm * (4 + 4 + 2) + 3 * d * 2) + 2 * MIB),
        name="ffn",
    )(*xs, mod, gpre, gpost, wgu, wgu, wd)


def _dense_post_kernel(*refs, n_a, n_ctx_tiles, k, res_w):
    a_refs = refs[:n_a]
    w_ref, x_ref, mod_ref, gpost_ref, o_ref, wbf_ref = refs[n_a:]

    @pl.when(pl.program_id(0) == 0)
    def _():
        wbf_ref[...] = w_ref[...].astype(BF16)

    def body(a_ref):
        tm = o_ref.shape[0]
        sub = min(tm, FFN_SUB_ROWS)
        for r in range(tm // sub):
            rows = slice(r * sub, (r + 1) * sub)
            y = _dot(a_ref[rows, :], wbf_ref[...])
            o_ref[rows, :] = _post_add(x_ref[rows, :], y, gpost_ref[...], mod_ref, k, res_w)

    _on_tile_ref(a_refs, pl.program_id(0), n_ctx_tiles, body)


def _dense_post(a_parts, w, x, mod, gpost, *, widx, k, tm, group_rows, n_ctx_groups):
    m, d = x.shape
    kk = a_parts[0].shape[1]
    n_ctx_tiles = n_ctx_groups * group_rows // tm
    midx = functools.partial(_mod_index, tm=tm, group_rows=group_rows, n_ctx_groups=n_ctx_groups)
    return pl.pallas_call(
        functools.partial(_dense_post_kernel, n_a=len(a_parts), n_ctx_tiles=n_ctx_tiles, k=k, res_w=1.0),
        grid=(m // tm,),
        in_specs=_split_row_specs(a_parts, tm, kk, n_ctx_tiles, 1) + [
            pl.BlockSpec((None, kk, d), lambda i: (widx, 0, 0), pipeline_mode=pl.Buffered(1)),
            pl.BlockSpec((tm, d), lambda i: (i, 0)),
            pl.BlockSpec((None, N_MOD, d), lambda i: (midx(i), 0, 0)),
            pl.BlockSpec((1, d), lambda i: (0, 0)),
        ],
        out_specs=pl.BlockSpec((tm, d), lambda i: (i, 0)),
        out_shape=jax.ShapeDtypeStruct((m, d), F32),
        scratch_shapes=[pltpu.VMEM((kk, d), BF16)],
        compiler_params=_params("arbitrary", vmem_bytes=2 * (len(a_parts) * tm * kk * 2 + 2 * tm * d * 4)
                                + kk * d * (4 + 2) + 2 * min(tm, FFN_SUB_ROWS) * d * 4 + 2 * MIB),
        name="dense_post",
    )(*a_parts, w, x, mod, gpost)


def _dft_tables(t, w):
    def cs(n):
        kn = np.outer(np.arange(n), np.arange(n)) % n
        ang = 2.0 * np.pi * kn.astype(np.float64) / n
        return np.cos(ang) / np.sqrt(n), np.sin(ang) / np.sqrt(n)

    ct, st = cs(t)
    cw, sw = cs(w)
    as_bf16 = lambda a: jnp.asarray(a, dtype=F32).astype(BF16)
    return as_bf16(np.concatenate([ct, st], axis=0)), as_bf16(cw), as_bf16(-sw)


def _fourier_kernel(x_ref, mod_ref, gpre_ref, cst_ref, cw_ref, swn_ref, o_ref, *, k):
    t = x_ref.shape[0]
    w = cw_ref.shape[0]
    h = _pre_mod(x_ref[...], gpre_ref[...], mod_ref, k).astype(BF16)
    cst = cst_ref[...]
    for g in range(FOURIER_GROUPS):
        p = _dot(cst, h[:, g * w:(g + 1) * w]).astype(BF16)
        f = _dot(p[:t], cw_ref[...]) + _dot(p[t:], swn_ref[...])
        o_ref[:, g * w:(g + 1) * w] = f.astype(BF16)


def _fourier(x, mod, gpre, *, t, nb, row_block0, mod0, k):
    d = x.shape[1]
    w = d // FOURIER_GROUPS
    cst, cw, swn = _dft_tables(t, w)
    return pl.pallas_call(
        functools.partial(_fourier_kernel, k=k),
        grid=(nb,),
        in_specs=[
            pl.BlockSpec((t, d), lambda b: (row_block0 + b, 0)),
            pl.BlockSpec((None, N_MOD, d), lambda b: (mod0(b), 0, 0)),
            pl.BlockSpec((1, d), lambda b: (0, 0)),
            pl.BlockSpec((2 * t, t), lambda b: (0, 0)),
            pl.BlockSpec((w, w), lambda b: (0, 0)),
            pl.BlockSpec((w, w), lambda b: (0, 0)),
        ],
        out_specs=pl.BlockSpec((t, d), lambda b: (b, 0)),
        out_shape=jax.ShapeDtypeStruct((nb * t, d), BF16),
        compiler_params=_params("arbitrary", vmem_bytes=2 * (t * d * 4 + 2 * t * t * 2 + 2 * w * w * 2
                                                             + t * d * 2) + t * d * 2 + 2 * t * w * 8),
        name="fourier",
    )(x, mod, gpre, cst, cw, swn)


def _rope_tables(t, head_k):
    rows = t // GRID_W
    pairs = head_k // 4
    row = np.repeat(np.arange(rows), GRID_W).astype(np.float64)
    col = np.tile(np.arange(GRID_W), rows).astype(np.float64)
    inv = ROPE_BASE ** (-np.arange(pairs, dtype=np.float64) / pairs)
    ang = np.concatenate([row[:, None] * inv, col[:, None] * inv], axis=-1)
    cos, sin = np.cos(ang), np.sin(ang)
    ce = np.repeat(cos, 2, axis=-1)
    se = np.stack([-sin, sin], axis=-1).reshape(t, head_k)
    return jnp.asarray(ce, dtype=F32), jnp.asarray(se, dtype=F32)


def _gla_in_kernel(x_ref, mod_ref, gpre_ref, w_ref, wlr_ref, wz_ref, bz_ref, cs_ref, ce_ref, se_ref,
                   p_ref, g_ref, h_ref, lr_ref, *, k, n_ctx_tiles, n_qk_steps, head_k):
    i = pl.program_id(0)
    j = pl.program_id(1)

    @pl.when(j == 0)
    def _():
        def prologue(rows):
            h = _pre_mod(x_ref[rows, :], gpre_ref[...], mod_ref, k).astype(BF16)
            h_ref[rows, :] = h
            lr_ref[rows, :] = _dot_nt(h, wlr_ref[...].astype(BF16)).astype(BF16)

        _for_row_blocks(x_ref.shape[0], prologue)

    def step(rope):
        z = _dot(lr_ref[...], wz_ref[...]) + bz_ref[...]
        g_ref[...] = (jnp.minimum(z, 0.0) - jnp.log1p(jnp.exp(-jnp.abs(z)))) * (LOG2_E / GATE_TAU)
        h = h_ref[...]
        for s in range(w_ref.shape[0] // head_k):
            cols = slice(s * head_k, (s + 1) * head_k)
            acc = _dot_nt(h, w_ref[cols, :].astype(BF16)) * cs_ref[:, cols]
            if rope:
                lane = lax.broadcasted_iota(jnp.int32, acc.shape, 1)
                partner = jnp.where(lane % 2 == 0, pltpu.roll(acc, head_k - 1, 1), pltpu.roll(acc, 1, 1))
                acc = acc * ce_ref[...] + partner * se_ref[...]
            p_ref[:, cols] = acc.astype(BF16)

    rope = jnp.logical_and(i >= n_ctx_tiles, j < n_qk_steps)
    pl.when(rope)(lambda: step(True))
    pl.when(jnp.logical_not(rope))(lambda: step(False))


def _gla_in(x, mod, gpre, w_in_t, wz, bz, colscale, *, widx, n_main, k, tm, n_steps, head_k, dk_tot,
            group_rows, n_ctx_groups):
    m, d = x.shape
    n = n_main
    tn = n // n_steps
    gcols = 2 * dk_tot // n_steps
    n_lr = w_in_t.shape[1] - n_main
    assert (4 * dk_tot) % tn == 0 and tn % head_k == 0 and gcols % LANES == 0 and n_main % n_lr == 0
    n_ctx_tiles = n_ctx_groups * group_rows // tm
    tiles_per_group = group_rows // tm
    ce, se = _rope_tables(group_rows, head_k)
    midx = functools.partial(_mod_index, tm=tm, group_rows=group_rows, n_ctx_groups=n_ctx_groups)

    def tab_idx(i, j):
        return (jnp.where(i >= n_ctx_tiles, (i - n_ctx_tiles) % tiles_per_group, 0), 0)

    return pl.pallas_call(
        functools.partial(_gla_in_kernel, k=k, n_ctx_tiles=n_ctx_tiles, n_qk_steps=4 * dk_tot // tn,
                          head_k=head_k),
        grid=(m // tm, n_steps),
        in_specs=[
            pl.BlockSpec((tm, d), lambda i, j: (i, 0)),
            pl.BlockSpec((None, N_MOD, d), lambda i, j: (midx(i), 0, 0)),
            pl.BlockSpec((1, d), lambda i, j: (0, 0)),
            pl.BlockSpec((None, tn, d), lambda i, j: (widx, j, 0)),
            pl.BlockSpec((None, n_lr, d), lambda i, j: (widx, n_main // n_lr, 0)),
            pl.BlockSpec((wz.shape[0], gcols), lambda i, j: (0, j)),
            pl.BlockSpec((1, gcols), lambda i, j: (0, j)),
            pl.BlockSpec((1, tn), lambda i, j: (0, j)),
            pl.BlockSpec((tm, head_k), tab_idx),
            pl.BlockSpec((tm, head_k), tab_idx),
        ],
        out_specs=[
            pl.BlockSpec((tm, tn), lambda i, j: (i, j)),
            pl.BlockSpec((tm, gcols), lambda i, j: (i, j)),
        ],
        out_shape=[
            jax.ShapeDtypeStruct((m, n), BF16),
            jax.ShapeDtypeStruct((m, 2 * dk_tot), F32),
        ],
        scratch_shapes=[pltpu.VMEM((tm, d), BF16), pltpu.VMEM((tm, n_lr), BF16)],
        compiler_params=_params("arbitrary", "arbitrary",
                                vmem_bytes=2 * (tm * d * 4 + d * tn * 4 + tm * tn * 2 + tm * gcols * 4
                                                + 2 * tm * head_k * 4) + tm * d * 2
                                + 2 * head_k * (tm * 4 + d * 2) + 2 * tm * gcols * 4 + 2 * MIB),
        name="gla_in",
    )(x, mod, gpre, w_in_t, w_in_t, wz, bz, colscale, ce, se)


def _scan_constants():
    c = CHUNK
    idx = np.arange(c)
    tri = (idx[None, :] <= idx[:, None]).astype(np.float64)
    masks = [np.eye(c)]
    for lev in range(N_LEVELS):
        h = 1 << lev
        start = idx // (2 * h) * (2 * h)
        upper = (idx - start) >= h
        same = start[:, None] == start[None, :]
        masks.append((same & upper[:, None] & ~upper[None, :]).astype(np.float64))
    m_f = np.stack(masks, axis=0)
    m_b = m_f[:, ::-1, ::-1]
    t_f = np.concatenate([tri, tri], axis=1)
    t_b = np.concatenate([tri[::-1, ::-1]] * 2, axis=1)
    return (jnp.asarray(t_f, dtype=BF16), jnp.asarray(t_b, dtype=BF16),
            jnp.asarray(m_f, dtype=F32), jnp.asarray(m_b, dtype=F32))


def _level_exponents(b, g, fwd):
    c, kk = b.shape
    tile = (c // SUBLANES, SUBLANES, kk)
    b8 = b.reshape(tile)
    r8 = lax.broadcasted_iota(jnp.int32, tile, 1)
    out = [jnp.where(r8 % 2 == (1 if fwd else 0), g.reshape(tile), 0.0).reshape(c, kk)]
    lo, hi = (1, 5) if fwd else (2, 6)
    mids = [jnp.where(r8 < 4, b8[:, lo:lo + 1, :], b8[:, hi:hi + 1, :]), b8[:, 3:4, :] if fwd else b8[:, 4:5, :]]
    for lev, mid in zip((1, 2), mids):
        h = 1 << lev
        second_half = (r8 % (2 * h)) >= h
        sign = jnp.where(second_half if fwd else jnp.logical_not(second_half), 1.0, -1.0)
        out.append(((b8 - mid) * sign).reshape(c, kk))
    for lev in range(3, N_LEVELS):
        h = 1 << lev
        bb = b.reshape(c // (2 * h), 2 * h, kk)
        first, second = bb[:, :h, :], bb[:, h:, :]
        if fwd:
            mid = bb[:, h - 1:h, :]
            t = jnp.concatenate([mid - first, second - mid], axis=1)
        else:
            mid = bb[:, h:h + 1, :]
            t = jnp.concatenate([first - mid, mid - second], axis=1)
        out.append(t.reshape(c, kk))
    return out


def _scan_chunks(chains):
    c = CHUNK

    st1 = []
    for q_ref, k_ref, g_ref, v_ref, tri_ref, m_ref, s_ref, o_ref, rows, kcols, vcols, fwd in chains:
        g = g_ref[rows, kcols]
        g_hi = g.astype(BF16)
        g_lo = (g - g_hi.astype(F32)).astype(BF16)
        b = _dot(tri_ref[...], jnp.concatenate([g_hi, g_lo], axis=0))
        diag = _dot_nt(q_ref[rows, kcols], k_ref[rows, kcols])
        st1.append((g, b, diag))

    st2 = []
    for (q_ref, k_ref, g_ref, v_ref, tri_ref, m_ref, s_ref, o_ref, rows, kcols, vcols, fwd), (g, b, diag) in zip(
            chains, st1):
        q_bf = q_ref[rows, kcols]
        k_bf = k_ref[rows, kcols]
        scores = m_ref[0] * diag
        for lev, t in enumerate(_level_exponents(b, g, fwd)):
            e = jnp.exp2(t).astype(BF16)
            scores += m_ref[lev + 1] * _dot_nt(q_bf * e, k_bf * e)
        st2.append(scores)

    for (q_ref, k_ref, g_ref, v_ref, tri_ref, m_ref, s_ref, o_ref, rows, kcols, vcols, fwd), (g, b, diag), scores in zip(
            chains, st1, st2):
        qe = q_ref[rows, kcols] * jnp.exp2(b).astype(BF16)
        o_ref[rows, vcols] = _dot(qe, s_ref[...].astype(BF16)) + _dot(scores.astype(BF16), v_ref[rows, vcols])

    for (q_ref, k_ref, g_ref, v_ref, tri_ref, m_ref, s_ref, o_ref, rows, kcols, vcols, fwd), (g, b, diag) in zip(
            chains, st1):
        last = c - 1 if fwd else 0
        b_last = b[last:last + 1]
        kt = k_ref[rows, kcols] * jnp.exp2(b_last - b).astype(BF16)
        e_col = jnp.broadcast_to(jnp.exp2(b_last), (LANES, b.shape[1])).T
        s = s_ref[...]
        decay = jnp.concatenate([e_col] * (s.shape[1] // LANES), axis=1)
        s_ref[...] = decay * s + _dot_tn(kt, v_ref[rows, vcols])


def _gla_scan_kernel(*refs, has_s0, has_sout, head_k, head_v):
    qf_ref, kf_ref, qb_ref, kb_ref, v_ref, r_ref, gf_ref, gb_ref = refs[:8]
    pos = 8
    if has_s0:
        s0_ref = refs[pos]
        pos += 1
    tf_ref, tb_ref, mf_ref, mb_ref, gn_ref = refs[pos:pos + 5]
    pos += 5
    o_ref = refs[pos]
    pos += 1
    if has_sout:
        sout_ref = refs[pos]
        pos += 1
    s_ref, of_ref, ob_ref = refs[pos:pos + 3]

    t = v_ref.shape[0]
    n = t // CHUNK
    hps = v_ref.shape[1] // head_v
    if has_s0:
        s_ref[...] = s0_ref[...]
    else:
        s_ref[...] = jnp.zeros_like(s_ref)

    def body(ci, carry):
        rows_f = pl.ds(pl.multiple_of(ci * CHUNK, CHUNK), CHUNK)
        rows_b = pl.ds(pl.multiple_of((n - 1 - ci) * CHUNK, CHUNK), CHUNK)
        chains = []
        for hh in range(hps):
            kcols = slice(hh * head_k, (hh + 1) * head_k)
            vcols = slice(hh * head_v, (hh + 1) * head_v)
            chains.append((qf_ref, kf_ref, gf_ref, v_ref, tf_ref, mf_ref, s_ref.at[0, hh], of_ref, rows_f,
                           kcols, vcols, True))
            chains.append((qb_ref, kb_ref, gb_ref, v_ref, tb_ref, mb_ref, s_ref.at[1, hh], ob_ref, rows_b,
                           kcols, vcols, False))
        _scan_chunks(chains)
        return carry

    lax.fori_loop(0, n, body, 0)

    if has_sout:
        sout_ref[...] = s_ref[...]

    def norm_body(ci, carry):
        rows = pl.ds(pl.multiple_of(ci * CHUNK, CHUNK), CHUNK)
        for hh in range(hps):
            vcols = slice(hh * head_v, (hh + 1) * head_v)
            o = of_ref[rows, vcols] + ob_ref[rows, vcols]
            r = r_ref[rows, vcols].astype(F32)
            o_ref[rows, vcols] = (_rms(o) * gn_ref[...] * (r * jax.nn.sigmoid(r))).astype(BF16)
        return carry

    lax.fori_loop(0, n, norm_body, 0)


def _gla_scan(p, g, gnorm, s0, *, t, nb, row_block0, heads, hps, head_k, head_v, want_state):
    assert heads % hps == 0
    nhb = heads // hps
    t_f, t_b, m_f, m_b = _scan_constants()
    qk_blk = lambda off: pl.BlockSpec((t, hps * head_k), lambda b, h: (row_block0 + b, off * nhb + h))
    v_off = 4 * heads * head_k // (hps * head_v)
    state_blk = pl.BlockSpec((None, None, 2, hps, head_k, head_v), lambda b, h: (b, 0, 0, h, 0, 0))
    in_specs = [
        qk_blk(0), qk_blk(1), qk_blk(2), qk_blk(3),
        pl.BlockSpec((t, hps * head_v), lambda b, h: (row_block0 + b, v_off + h)),
        pl.BlockSpec((t, hps * head_v), lambda b, h: (row_block0 + b, v_off + nhb + h)),
        pl.BlockSpec((t, hps * head_k), lambda b, h: (row_block0 + b, h)),
        pl.BlockSpec((t, hps * head_k), lambda b, h: (row_block0 + b, nhb + h)),
    ]
    args = [p, p, p, p, p, p, g, g]
    if s0 is not None:
        in_specs.append(state_blk)
        args.append(s0)
    in_specs += [
        pl.BlockSpec(t_f.shape, lambda b, h: (0, 0)),
        pl.BlockSpec(t_b.shape, lambda b, h: (0, 0)),
        pl.BlockSpec(m_f.shape, lambda b, h: (0, 0, 0)),
        pl.BlockSpec(m_b.shape, lambda b, h: (0, 0, 0)),
        pl.BlockSpec((1, head_v), lambda b, h: (0, 0)),
    ]
    args += [t_f, t_b, m_f, m_b, gnorm]
    out_specs = [pl.BlockSpec((t, hps * head_v), lambda b, h: (b, h))]
    out_shape = [jax.ShapeDtypeStruct((nb * t, heads * head_v), BF16)]
    if want_state:
        out_specs.append(state_blk)
        out_shape.append(jax.ShapeDtypeStruct((nb, 1, 2, heads, head_k, head_v), F32))
    return pl.pallas_call(
        functools.partial(_gla_scan_kernel, has_s0=s0 is not None, has_sout=want_state, head_k=head_k,
                          head_v=head_v),
        grid=(nb, nhb),
        in_specs=in_specs,
        out_specs=out_specs,
        out_shape=out_shape,
        scratch_shapes=[
            pltpu.VMEM((2, hps, head_k, head_v), F32),
            pltpu.VMEM((t, hps * head_v), F32),
            pltpu.VMEM((t, hps * head_v), F32),
        ],
        compiler_params=_params(
            "arbitrary", "arbitrary",
            vmem_bytes=2 * t * hps * (4 * head_k * 2 + 3 * head_v * 2 + 2 * head_k * 4)
            + (2 * (int(s0 is not None) + int(want_state)) + 1) * 2 * hps * head_k * head_v * 4
            + 2 * t * hps * head_v * 4 + 8 * MIB),
        name="gla_scan",
    )(*args)


def kernel(x_prompt, x_sample, state_gla, c, c_ctx, ada_w, ada_b, norm_pre, norm_post, ffn_w_gate_up,
           ffn_w_down, fourier_w, gla_w_in, gla_w_gate_up, gla_b_gate, gla_norm, gla_w_out):
    batch, seq, d = x_prompt.shape
    dec_batch, dec_seq, _ = x_sample.shape
    depth = ada_w.shape[0]
    heads, head_k, head_v = state_gla.shape[3:]
    dk_tot = heads * head_k
    dv_tot = heads * head_v
    rank = gla_w_gate_up.shape[2]
    d_ff = ffn_w_down.shape[2]
    ctx_rows = batch * seq
    assert ctx_rows % dec_seq == 0 and dec_seq % seq == 0 and 1 + dec_batch <= SUBLANES
    n_ctx_groups = ctx_rows // dec_seq
    grp = dict(group_rows=dec_seq, n_ctx_groups=n_ctx_groups)
    tm_small = min(dec_seq, 512)
    tm_big = min(dec_seq, 1024)
    tf = 512 if d_ff % 512 == 0 else d_ff

    cond8 = jnp.zeros((SUBLANES, d), F32).at[0].set(c_ctx).at[1:1 + dec_batch].set(c)
    mods = _modulation(cond8, ada_w, ada_b)[:, :1 + dec_batch].reshape(depth, 1 + dec_batch, N_MOD, d)

    n_main = 4 * dk_tot + 2 * dv_tot
    w_in_t = jnp.swapaxes(gla_w_in, 1, 2)

    xs = (x_prompt.reshape(ctx_rows, d), x_sample.reshape(dec_batch * dec_seq, d))
    states = []
    for l in range(depth):
        mod = mods[l]
        j = l // 2
        npre = lambda s: norm_pre[l, s][None, :]
        npost = lambda s: norm_post[l, s][None, :]
        ffn = functools.partial(_ffn, mod=mod, wgu=ffn_w_gate_up, wd=ffn_w_down, layer=l, tf=tf, tm=tm_big,
                                **grp)
        (x,) = ffn(xs, gpre=npre(0), gpost=npost(0), slot=0, k=0, split_out=False)
        if l % 2 == 0:
            f_ctx = _fourier(x, mod, npre(1), t=seq, nb=batch, row_block0=0, mod0=lambda b: 0, k=1)
            f_lat = _fourier(x, mod, npre(1), t=dec_seq, nb=dec_batch, row_block0=n_ctx_groups,
                             mod0=lambda b: 1 + b, k=1)
            x = _dense_post((f_ctx, f_lat), fourier_w, x, mod, npost(1), widx=j, k=1, tm=tm_small, **grp)
        else:
            zero = jnp.zeros((rank, dk_tot), F32)
            wz = jnp.concatenate([jnp.concatenate([gla_w_gate_up[j, 0], zero], axis=1),
                                  jnp.concatenate([zero, gla_w_gate_up[j, 1]], axis=1)], axis=0).astype(BF16)
            bz = gla_b_gate[j].reshape(1, 2 * dk_tot)
            colscale = jnp.ones((n_main,), F32).at[:dk_tot].set(head_k ** -0.5)
            colscale = colscale.at[2 * dk_tot:3 * dk_tot].set(head_k ** -0.5).reshape(1, n_main)
            p, g = _gla_in(x, mod, npre(1), w_in_t, wz, bz, colscale, widx=j, n_main=n_main, k=1,
                           tm=tm_big, n_steps=8, head_k=head_k, dk_tot=dk_tot, **grp)
            gn = gla_norm[j][None, :]
            scan = functools.partial(_gla_scan, p, g, gn, heads=heads, head_k=head_k, head_v=head_v)
            o_ctx, st = scan(None, t=seq, nb=batch, row_block0=0, hps=heads, want_state=True)
            (o_lat,) = scan(state_gla[:, j:j + 1], t=dec_seq, nb=dec_batch, row_block0=n_ctx_groups,
                            hps=min(heads, max(1, 2 * heads * seq // dec_seq)), want_state=False)
            states.append(st)
            x = _dense_post((o_ctx, o_lat), gla_w_out, x, mod, npost(1), widx=j, k=1, tm=tm_small, **grp)
        xs = tuple(ffn((x,), gpre=npre(2), gpost=npost(2), slot=1, k=2, split_out=l == depth - 1))

    y_prompt = xs[0].reshape(batch, seq, d)
    y_sample = xs[1].reshape(dec_batch, dec_seq, d)
    new_state = jnp.concatenate(states, axis=1)
    return (y_prompt, y_sample, new_state)
```

```python
import functools

import numpy as np
import jax
import jax.numpy as jnp
from jax import lax
from jax.experimental import pallas as pl
from jax.experimental.pallas import tpu as pltpu

F32 = jnp.float32
BF16 = jnp.bfloat16

EPS = 1e-6
N_MOD = 9
GRID_W = 64
FOURIER_GROUPS = 4
GATE_TAU = 16.0
CHUNK = 64
ROPE_BASE = 10000.0
LOG2_E = 1.4426950408889634
N_LEVELS = 6
assert CHUNK == 1 << N_LEVELS

V7X_VMEM_BYTES = 64 * 1024 * 1024
MIB = 1024 * 1024
LANES = 128
SUBLANES = 8


def _params(*sem, vmem_bytes):
    request = (int(vmem_bytes) // MIB + 2) * MIB
    assert request < V7X_VMEM_BYTES, request
    return pltpu.CompilerParams(dimension_semantics=sem, vmem_limit_bytes=request)


def _dot(a, b):
    return jnp.dot(a, b, preferred_element_type=F32)


def _dot_nt(a, b):
    return lax.dot_general(a, b, (((1,), (1,)), ((), ())), preferred_element_type=F32)


def _dot_tn(a, b):
    return lax.dot_general(a, b, (((0,), (0,)), ((), ())), preferred_element_type=F32)


def _rms(x):
    return x * lax.rsqrt(jnp.mean(x * x, axis=-1, keepdims=True) + EPS)


def _pre_mod(x, gpre, mod_ref, k):
    gs = gpre * (1.0 + mod_ref[3 * k + 1:3 * k + 2, :])
    return _rms(x) * gs + mod_ref[3 * k:3 * k + 1, :]


def _post_add(x, out, gpost, mod_ref, k, res_w):
    gg = (res_w * mod_ref[3 * k + 2:3 * k + 3, :]) * gpost
    return x + _rms(out) * gg


ROW_BLOCK = 128


def _for_row_blocks(n_rows, fn):
    block = min(ROW_BLOCK, n_rows)

    def body(r, carry):
        fn(pl.ds(pl.multiple_of(r * block, block), block))
        return carry

    lax.fori_loop(0, n_rows // block, body, 0)


def _mod_index(i, tm, group_rows, n_ctx_groups):
    return jnp.maximum(i * tm // group_rows - (n_ctx_groups - 1), 0)


def _on_tile_ref(refs, i, n_ctx_tiles, fn):
    if len(refs) == 1:
        fn(refs[0])
    else:
        pl.when(i < n_ctx_tiles)(lambda: fn(refs[0]))
        pl.when(i >= n_ctx_tiles)(lambda: fn(refs[1]))


def _split_row_specs(arrays, tm, width, n_ctx_tiles, ngrid, buffers=None):
    kw = {} if buffers is None else dict(pipeline_mode=pl.Buffered(buffers))
    if ngrid == 1:
        wrap = lambda f: (lambda i: f(i))
    else:
        wrap = lambda f: (lambda i, j: f(i))
    if len(arrays) == 1:
        return [pl.BlockSpec((tm, width), wrap(lambda i: (i, 0)), **kw)]
    return [
        pl.BlockSpec((tm, width), wrap(lambda i: (jnp.minimum(i, n_ctx_tiles - 1), 0)), **kw),
        pl.BlockSpec((tm, width), wrap(lambda i: (jnp.maximum(i - n_ctx_tiles, 0), 0)), **kw),
    ]


def _mod_kernel(c_ref, w_ref, b_ref, o_ref):
    c = c_ref[...]
    s = (c * jax.nn.sigmoid(c)).astype(BF16)
    o_ref[...] = _dot(s, w_ref[...].astype(BF16)) + b_ref[...]


def _modulation(cond8, ada_w, ada_b):
    depth, d, n = ada_w.shape
    tn = min(n, 1024)
    return pl.pallas_call(
        _mod_kernel,
        grid=(depth, n // tn),
        in_specs=[
            pl.BlockSpec((SUBLANES, d), lambda l, j: (0, 0)),
            pl.BlockSpec((None, d, tn), lambda l, j: (l, 0, j)),
            pl.BlockSpec((None, 1, tn), lambda l, j: (l, 0, j)),
        ],
        out_specs=pl.BlockSpec((None, SUBLANES, tn), lambda l, j: (l, 0, j)),
        out_shape=jax.ShapeDtypeStruct((depth, SUBLANES, n), F32),
        compiler_params=_params("arbitrary", "arbitrary", vmem_bytes=d * tn * (2 * 4 + 2) + 2 * MIB),
        name="modulation",
    )(cond8, ada_w, ada_b.reshape(depth, 1, n))


def _on_tile_part(parts, tile, n_ctx_tiles, fn):
    if len(parts) == 1:
        fn(parts[0], tile)
    else:
        pl.when(tile < n_ctx_tiles)(lambda: fn(parts[0], tile))
        pl.when(tile >= n_ctx_tiles)(lambda: fn(parts[1], tile - n_ctx_tiles))


def _ffn_kernel(*refs, n_x, n_out, n_ctx_tiles, k, res_w):
    x_hbm = refs[:n_x]
    mod_ref, gpre_ref, gpost_ref, wg_ref, wu_ref, wd_ref = refs[n_x:n_x + 6]
    o_hbm = refs[n_x + 6:n_x + 6 + n_out]
    h_ref, x_buf, acc_ref, out_sem, x_sem = refs[n_x + 6 + n_out:]
    i = pl.program_id(0)
    j = pl.program_id(1)
    tm = x_buf.shape[0]
    n_blk = x_sem.shape[0]
    blk = tm // n_blk

    def out_copy(ref, tile):
        rows = pl.ds(pl.multiple_of(tile * tm, tm), tm)
        return pltpu.make_async_copy(acc_ref, ref.at[rows, :], out_sem)

    def x_copy(ref, tile, b):
        src = pl.ds(pl.multiple_of(tile * tm + b * blk, blk), blk)
        dst = pl.ds(pl.multiple_of(b * blk, blk), blk)
        return pltpu.make_async_copy(ref.at[src, :], x_buf.at[dst, :], x_sem.at[b])

    @pl.when(j == 0)
    def _():
        def fetch_and_prologue(ref, t):
            for b in range(n_blk):
                x_copy(ref, t, b).start()

            def body(b, carry):
                x_copy(ref, t, b).wait()
                rows = pl.ds(pl.multiple_of(b * blk, blk), blk)
                h_ref[rows, :] = _pre_mod(x_buf[rows, :], gpre_ref[...], mod_ref, k).astype(BF16)
                return carry

            lax.fori_loop(0, n_blk, body, 0)

        _on_tile_part(x_hbm, i, n_ctx_tiles, fetch_and_prologue)

        @pl.when(i > 0)
        def _():
            _on_tile_part(o_hbm, i - 1, n_ctx_tiles, lambda ref, t: out_copy(ref, t).wait())

        acc_ref[...] = jnp.zeros_like(acc_ref)

    h = h_ref[...]
    half = wg_ref.shape[1] // 2
    acts = []
    for c0 in (0, half):
        g = _dot(h, wg_ref[:, c0:c0 + half].astype(BF16))
        u = _dot(h, wu_ref[:, c0:c0 + half].astype(BF16))
        acts.append((g * jax.nn.sigmoid(g) * u).astype(BF16))
    acc_ref[...] += _dot(jnp.concatenate(acts, axis=1), wd_ref[...].astype(BF16))

    @pl.when(j == pl.num_programs(1) - 1)
    def _():
        def epilogue(rows):
            acc_ref[rows, :] = _post_add(x_buf[rows, :], acc_ref[rows, :], gpost_ref[...], mod_ref, k, res_w)

        _for_row_blocks(tm, epilogue)
        _on_tile_part(o_hbm, i, n_ctx_tiles, lambda ref, t: out_copy(ref, t).start())

        @pl.when(i == pl.num_programs(0) - 1)
        def _():
            _on_tile_part(o_hbm, i, n_ctx_tiles, lambda ref, t: out_copy(ref, t).wait())


def _ffn(xs, mod, gpre, gpost, wgu, wd, *, layer, slot, k, tm, tf, split_out, group_rows, n_ctx_groups):
    m = sum(x.shape[0] for x in xs)
    d = xs[0].shape[1]
    nf = wd.shape[2] // tf
    n_ctx_tiles = n_ctx_groups * group_rows // tm
    out_rows = (n_ctx_tiles * tm, m - n_ctx_tiles * tm) if split_out else (m,)
    midx = functools.partial(_mod_index, tm=tm, group_rows=group_rows, n_ctx_groups=n_ctx_groups)
    hbm = pl.BlockSpec(memory_space=pl.ANY)
    return pl.pallas_call(
        functools.partial(_ffn_kernel, n_x=len(xs), n_out=len(out_rows), n_ctx_tiles=n_ctx_tiles, k=k,
                          res_w=0.5),
        grid=(m // tm, nf),
        in_specs=[hbm] * len(xs) + [
            pl.BlockSpec((None, N_MOD, d), lambda i, j: (midx(i), 0, 0)),
            pl.BlockSpec((1, d), lambda i, j: (0, 0)),
            pl.BlockSpec((1, d), lambda i, j: (0, 0)),
            pl.BlockSpec((None, None, d, tf), lambda i, j: (layer, slot, 0, j)),
            pl.BlockSpec((None, None, d, tf), lambda i, j: (layer, slot, 0, j + nf)),
            pl.BlockSpec((None, None, tf, d), lambda i, j: (layer, slot, j, 0)),
        ],
        out_specs=[hbm] * len(out_rows),
        out_shape=[jax.ShapeDtypeStruct((r, d), F32) for r in out_rows],
        scratch_shapes=[
            pltpu.VMEM((tm, d), BF16),
            pltpu.VMEM((tm, d), F32),
            pltpu.VMEM((tm, d), F32),
            pltpu.SemaphoreType.DMA(()),
            pltpu.SemaphoreType.DMA((tm // min(ROW_BLOCK, tm),)),
        ],
        compiler_params=_params("arbitrary", "arbitrary",
                                vmem_bytes=tm * d * (2 + 4 + 4) + 2 * 3 * d * tf * 4
                                + (tf // 2) * (tm * (4 + 4 + 2) + 3 * d * 2) + 2 * MIB),
        name="ffn",
    )(*xs, mod, gpre, gpost, wgu, wgu, wd)


DENSE_SUB_ROWS = 256


def _dense_post_kernel(*refs, n_a, n_ctx_tiles, k, res_w):
    a_refs = refs[:n_a]
    w_ref, x_ref, mod_ref, gpost_ref, o_ref, wbf_ref = refs[n_a:]

    @pl.when(pl.program_id(0) == 0)
    def _():
        wbf_ref[...] = w_ref[...].astype(BF16)

    def body(a_ref):
        tm = o_ref.shape[0]
        sub = min(tm, DENSE_SUB_ROWS)
        for r in range(tm // sub):
            rows = slice(r * sub, (r + 1) * sub)
            y = _dot(a_ref[rows, :], wbf_ref[...])
            o_ref[rows, :] = _post_add(x_ref[rows, :], y, gpost_ref[...], mod_ref, k, res_w)

    _on_tile_ref(a_refs, pl.program_id(0), n_ctx_tiles, body)


def _dense_post(a_parts, w, x, mod, gpost, *, widx, k, tm, group_rows, n_ctx_groups):
    m, d = x.shape
    kk = a_parts[0].shape[1]
    n_ctx_tiles = n_ctx_groups * group_rows // tm
    midx = functools.partial(_mod_index, tm=tm, group_rows=group_rows, n_ctx_groups=n_ctx_groups)
    return pl.pallas_call(
        functools.partial(_dense_post_kernel, n_a=len(a_parts), n_ctx_tiles=n_ctx_tiles, k=k, res_w=1.0),
        grid=(m // tm,),
        in_specs=_split_row_specs(a_parts, tm, kk, n_ctx_tiles, 1) + [
            pl.BlockSpec((None, kk, d), lambda i: (widx, 0, 0), pipeline_mode=pl.Buffered(1)),
            pl.BlockSpec((tm, d), lambda i: (i, 0)),
            pl.BlockSpec((None, N_MOD, d), lambda i: (midx(i), 0, 0)),
            pl.BlockSpec((1, d), lambda i: (0, 0)),
        ],
        out_specs=pl.BlockSpec((tm, d), lambda i: (i, 0)),
        out_shape=jax.ShapeDtypeStruct((m, d), F32),
        scratch_shapes=[pltpu.VMEM((kk, d), BF16)],
        compiler_params=_params("arbitrary", vmem_bytes=2 * (len(a_parts) * tm * kk * 2 + 2 * tm * d * 4)
                                + kk * d * (4 + 2) + 2 * min(tm, DENSE_SUB_ROWS) * d * 4 + 2 * MIB),
        name="dense_post",
    )(*a_parts, w, x, mod, gpost)


def _dft_tables(t, w):
    def cs(n):
        kn = np.outer(np.arange(n), np.arange(n)) % n
        ang = 2.0 * np.pi * kn.astype(np.float64) / n
        return np.cos(ang) / np.sqrt(n), np.sin(ang) / np.sqrt(n)

    ct, st = cs(t)
    cw, sw = cs(w)
    as_bf16 = lambda a: jnp.asarray(a, dtype=F32).astype(BF16)
    return as_bf16(np.concatenate([ct, st], axis=0)), as_bf16(cw), as_bf16(-sw)


def _fourier_kernel(x_ref, mod_ref, gpre_ref, cst_ref, cw_ref, swn_ref, o_ref, *, k):
    t = x_ref.shape[0]
    w = cw_ref.shape[0]
    h = _pre_mod(x_ref[...], gpre_ref[...], mod_ref, k).astype(BF16)
    cst = cst_ref[...]
    for g in range(FOURIER_GROUPS):
        p = _dot(cst, h[:, g * w:(g + 1) * w]).astype(BF16)
        f = _dot(p[:t], cw_ref[...]) + _dot(p[t:], swn_ref[...])
        o_ref[:, g * w:(g + 1) * w] = f.astype(BF16)


def _fourier(x, mod, gpre, *, t, nb, row_block0, mod0, k):
    d = x.shape[1]
    w = d // FOURIER_GROUPS
    cst, cw, swn = _dft_tables(t, w)
    return pl.pallas_call(
        functools.partial(_fourier_kernel, k=k),
        grid=(nb,),
        in_specs=[
            pl.BlockSpec((t, d), lambda b: (row_block0 + b, 0)),
            pl.BlockSpec((None, N_MOD, d), lambda b: (mod0(b), 0, 0)),
            pl.BlockSpec((1, d), lambda b: (0, 0)),
            pl.BlockSpec((2 * t, t), lambda b: (0, 0)),
            pl.BlockSpec((w, w), lambda b: (0, 0)),
            pl.BlockSpec((w, w), lambda b: (0, 0)),
        ],
        out_specs=pl.BlockSpec((t, d), lambda b: (b, 0)),
        out_shape=jax.ShapeDtypeStruct((nb * t, d), BF16),
        compiler_params=_params("arbitrary", vmem_bytes=2 * (t * d * 4 + 2 * t * t * 2 + 2 * w * w * 2
                                                             + t * d * 2) + t * d * 2 + 2 * t * w * 8),
        name="fourier",
    )(x, mod, gpre, cst, cw, swn)


def _rope_tables(t, head_k):
    rows = t // GRID_W
    pairs = head_k // 4
    row = np.repeat(np.arange(rows), GRID_W).astype(np.float64)
    col = np.tile(np.arange(GRID_W), rows).astype(np.float64)
    inv = ROPE_BASE ** (-np.arange(pairs, dtype=np.float64) / pairs)
    ang = np.concatenate([row[:, None] * inv, col[:, None] * inv], axis=-1)
    cos, sin = np.cos(ang), np.sin(ang)
    ce = np.repeat(cos, 2, axis=-1)
    se = np.stack([-sin, sin], axis=-1).reshape(t, head_k)
    return jnp.asarray(ce, dtype=F32), jnp.asarray(se, dtype=F32)


def _gla_in_kernel(x_ref, mod_ref, gpre_ref, w_ref, wlr_ref, wz_ref, bz_ref, cs_ref, ce_ref, se_ref,
                   p_ref, g_ref, h_ref, lr_ref, *, k, n_ctx_tiles, n_qk_steps, head_k):
    i = pl.program_id(0)
    j = pl.program_id(1)

    @pl.when(j == 0)
    def _():
        def prologue(rows):
            h = _pre_mod(x_ref[rows, :], gpre_ref[...], mod_ref, k).astype(BF16)
            h_ref[rows, :] = h
            lr_ref[rows, :] = _dot_nt(h, wlr_ref[...].astype(BF16)).astype(BF16)

        _for_row_blocks(x_ref.shape[0], prologue)

    def step(rope):
        z = _dot(lr_ref[...], wz_ref[...]) + bz_ref[...]
        g_ref[...] = (jnp.minimum(z, 0.0) - jnp.log1p(jnp.exp(-jnp.abs(z)))) * (LOG2_E / GATE_TAU)
        h = h_ref[...]
        for s in range(w_ref.shape[0] // head_k):
            cols = slice(s * head_k, (s + 1) * head_k)
            acc = _dot_nt(h, w_ref[cols, :].astype(BF16)) * cs_ref[:, cols]
            if rope:
                lane = lax.broadcasted_iota(jnp.int32, acc.shape, 1)
                partner = jnp.where(lane % 2 == 0, pltpu.roll(acc, head_k - 1, 1), pltpu.roll(acc, 1, 1))
                acc = acc * ce_ref[...] + partner * se_ref[...]
            p_ref[:, cols] = acc.astype(BF16)

    rope = jnp.logical_and(i >= n_ctx_tiles, j < n_qk_steps)
    pl.when(rope)(lambda: step(True))
    pl.when(jnp.logical_not(rope))(lambda: step(False))


def _gla_in(x, mod, gpre, w_in_t, wz, bz, colscale, *, widx, n_main, k, tm, n_steps, head_k, dk_tot,
            group_rows, n_ctx_groups):
    m, d = x.shape
    n = n_main
    tn = n // n_steps
    gcols = 2 * dk_tot // n_steps
    n_lr = w_in_t.shape[1] - n_main
    assert (4 * dk_tot) % tn == 0 and tn % head_k == 0 and gcols % LANES == 0 and n_main % n_lr == 0
    n_ctx_tiles = n_ctx_groups * group_rows // tm
    tiles_per_group = group_rows // tm
    ce, se = _rope_tables(group_rows, head_k)
    midx = functools.partial(_mod_index, tm=tm, group_rows=group_rows, n_ctx_groups=n_ctx_groups)

    def tab_idx(i, j):
        return (jnp.where(i >= n_ctx_tiles, (i - n_ctx_tiles) % tiles_per_group, 0), 0)

    return pl.pallas_call(
        functools.partial(_gla_in_kernel, k=k, n_ctx_tiles=n_ctx_tiles, n_qk_steps=4 * dk_tot // tn,
                          head_k=head_k),
        grid=(m // tm, n_steps),
        in_specs=[
            pl.BlockSpec((tm, d), lambda i, j: (i, 0)),
            pl.BlockSpec((None, N_MOD, d), lambda i, j: (midx(i), 0, 0)),
            pl.BlockSpec((1, d), lambda i, j: (0, 0)),
            pl.BlockSpec((None, tn, d), lambda i, j: (widx, j, 0)),
            pl.BlockSpec((None, n_lr, d), lambda i, j: (widx, n_main // n_lr, 0)),
            pl.BlockSpec((wz.shape[0], gcols), lambda i, j: (0, j)),
            pl.BlockSpec((1, gcols), lambda i, j: (0, j)),
            pl.BlockSpec((1, tn), lambda i, j: (0, j)),
            pl.BlockSpec((tm, head_k), tab_idx),
            pl.BlockSpec((tm, head_k), tab_idx),
        ],
        out_specs=[
            pl.BlockSpec((tm, tn), lambda i, j: (i, j)),
            pl.BlockSpec((tm, gcols), lambda i, j: (i, j)),
        ],
        out_shape=[
            jax.ShapeDtypeStruct((m, n), BF16),
            jax.ShapeDtypeStruct((m, 2 * dk_tot), F32),
        ],
        scratch_shapes=[pltpu.VMEM((tm, d), BF16), pltpu.VMEM((tm, n_lr), BF16)],
        compiler_params=_params("arbitrary", "arbitrary",
                                vmem_bytes=2 * (tm * d * 4 + d * tn * 4 + tm * tn * 2 + tm * gcols * 4
                                                + 2 * tm * head_k * 4) + tm * d * 2
                                + 2 * head_k * (tm * 4 + d * 2) + 2 * tm * gcols * 4 + 2 * MIB),
        name="gla_in",
    )(x, mod, gpre, w_in_t, w_in_t, wz, bz, colscale, ce, se)


def _scan_constants():
    c = CHUNK
    idx = np.arange(c)
    tri = (idx[None, :] <= idx[:, None]).astype(np.float64)
    masks = [np.eye(c)]
    for lev in range(N_LEVELS):
        h = 1 << lev
        start = idx // (2 * h) * (2 * h)
        upper = (idx - start) >= h
        same = start[:, None] == start[None, :]
        masks.append((same & upper[:, None] & ~upper[None, :]).astype(np.float64))
    m_f = np.stack(masks, axis=0)
    m_b = m_f[:, ::-1, ::-1]
    t_f = np.concatenate([tri, tri], axis=1)
    t_b = np.concatenate([tri[::-1, ::-1]] * 2, axis=1)
    return (jnp.asarray(t_f, dtype=BF16), jnp.asarray(t_b, dtype=BF16),
            jnp.asarray(m_f, dtype=F32), jnp.asarray(m_b, dtype=F32))


def _level_exponents(b, g, fwd):
    c, kk = b.shape
    tile = (c // SUBLANES, SUBLANES, kk)
    b8 = b.reshape(tile)
    r8 = lax.broadcasted_iota(jnp.int32, tile, 1)
    out = [jnp.where(r8 % 2 == (1 if fwd else 0), g.reshape(tile), 0.0).reshape(c, kk)]
    lo, hi = (1, 5) if fwd else (2, 6)
    mids = [jnp.where(r8 < 4, b8[:, lo:lo + 1, :], b8[:, hi:hi + 1, :]), b8[:, 3:4, :] if fwd else b8[:, 4:5, :]]
    for lev, mid in zip((1, 2), mids):
        h = 1 << lev
        second_half = (r8 % (2 * h)) >= h
        sign = jnp.where(second_half if fwd else jnp.logical_not(second_half), 1.0, -1.0)
        out.append(((b8 - mid) * sign).reshape(c, kk))
    for lev in range(3, N_LEVELS):
        h = 1 << lev
        bb = b.reshape(c // (2 * h), 2 * h, kk)
        first, second = bb[:, :h, :], bb[:, h:, :]
        if fwd:
            mid = bb[:, h - 1:h, :]
            t = jnp.concatenate([mid - first, second - mid], axis=1)
        else:
            mid = bb[:, h:h + 1, :]
            t = jnp.concatenate([first - mid, mid - second], axis=1)
        out.append(t.reshape(c, kk))
    return out


def _scan_chunks(chains):
    c = CHUNK

    st1 = []
    for q_ref, k_ref, g_ref, v_ref, tri_ref, m_ref, s_ref, o_ref, rows, kcols, vcols, fwd in chains:
        g = g_ref[rows, kcols]
        g_hi = g.astype(BF16)
        g_lo = (g - g_hi.astype(F32)).astype(BF16)
        b = _dot(tri_ref[...], jnp.concatenate([g_hi, g_lo], axis=0))
        diag = _dot_nt(q_ref[rows, kcols], k_ref[rows, kcols])
        st1.append((g, b, diag))

    st2 = []
    for (q_ref, k_ref, g_ref, v_ref, tri_ref, m_ref, s_ref, o_ref, rows, kcols, vcols, fwd), (g, b, diag) in zip(
            chains, st1):
        q_bf = q_ref[rows, kcols]
        k_bf = k_ref[rows, kcols]
        scores = m_ref[0] * diag
        for lev, t in enumerate(_level_exponents(b, g, fwd)):
            e = jnp.exp2(t).astype(BF16)
            scores += m_ref[lev + 1] * _dot_nt(q_bf * e, k_bf * e)
        st2.append(scores)

    for (q_ref, k_ref, g_ref, v_ref, tri_ref, m_ref, s_ref, o_ref, rows, kcols, vcols, fwd), (g, b, diag), scores in zip(
            chains, st1, st2):
        qe = q_ref[rows, kcols] * jnp.exp2(b).astype(BF16)
        o_ref[rows, vcols] = _dot(qe, s_ref[...].astype(BF16)) + _dot(scores.astype(BF16), v_ref[rows, vcols])

    for (q_ref, k_ref, g_ref, v_ref, tri_ref, m_ref, s_ref, o_ref, rows, kcols, vcols, fwd), (g, b, diag) in zip(
            chains, st1):
        last = c - 1 if fwd else 0
        b_last = b[last:last + 1]
        kt = k_ref[rows, kcols] * jnp.exp2(b_last - b).astype(BF16)
        e_col = jnp.broadcast_to(jnp.exp2(b_last), (LANES, b.shape[1])).T
        s = s_ref[...]
        decay = jnp.concatenate([e_col] * (s.shape[1] // LANES), axis=1)
        s_ref[...] = decay * s + _dot_tn(kt, v_ref[rows, vcols])


def _gla_scan_kernel(*refs, has_s0, has_sout, head_k, head_v):
    qf_ref, kf_ref, qb_ref, kb_ref, v_ref, r_ref, gf_ref, gb_ref = refs[:8]
    pos = 8
    if has_s0:
        s0_ref = refs[pos]
        pos += 1
    tf_ref, tb_ref, mf_ref, mb_ref, gn_ref = refs[pos:pos + 5]
    pos += 5
    o_ref = refs[pos]
    pos += 1
    if has_sout:
        sout_ref = refs[pos]
        pos += 1
    s_ref, of_ref, ob_ref = refs[pos:pos + 3]
    if has_sout:
        s_ref = sout_ref

    t = v_ref.shape[0]
    n = t // CHUNK
    hps = v_ref.shape[1] // head_v
    if has_s0:
        s_ref[...] = s0_ref[...]
    else:
        s_ref[...] = jnp.zeros_like(s_ref)

    def body(ci, carry):
        rows_f = pl.ds(pl.multiple_of(ci * CHUNK, CHUNK), CHUNK)
        rows_b = pl.ds(pl.multiple_of((n - 1 - ci) * CHUNK, CHUNK), CHUNK)
        chains = []
        for hh in range(hps):
            kcols = slice(hh * head_k, (hh + 1) * head_k)
            vcols = slice(hh * head_v, (hh + 1) * head_v)
            chains.append((qf_ref, kf_ref, gf_ref, v_ref, tf_ref, mf_ref, s_ref.at[0, hh], of_ref, rows_f,
                           kcols, vcols, True))
            chains.append((qb_ref, kb_ref, gb_ref, v_ref, tb_ref, mb_ref, s_ref.at[1, hh], ob_ref, rows_b,
                           kcols, vcols, False))
        _scan_chunks(chains)
        return carry

    lax.fori_loop(0, n, body, 0)

    def norm_body(ci, carry):
        rows = pl.ds(pl.multiple_of(ci * CHUNK, CHUNK), CHUNK)
        for hh in range(hps):
            vcols = slice(hh * head_v, (hh + 1) * head_v)
            o = of_ref[rows, vcols] + ob_ref[rows, vcols]
            r = r_ref[rows, vcols].astype(F32)
            o_ref[rows, vcols] = (_rms(o) * gn_ref[...] * (r * jax.nn.sigmoid(r))).astype(BF16)
        return carry

    lax.fori_loop(0, n, norm_body, 0)


def _gla_scan(p, g, gnorm, s0, *, t, nb, row_block0, heads, hps, head_k, head_v, want_state):
    assert heads % hps == 0
    nhb = heads // hps
    t_f, t_b, m_f, m_b = _scan_constants()
    qk_blk = lambda off: pl.BlockSpec((t, hps * head_k), lambda b, h: (row_block0 + b, off * nhb + h))
    v_off = 4 * heads * head_k // (hps * head_v)
    state_blk = pl.BlockSpec((None, None, 2, hps, head_k, head_v), lambda b, h: (b, 0, 0, h, 0, 0))
    in_specs = [
        qk_blk(0), qk_blk(1), qk_blk(2), qk_blk(3),
        pl.BlockSpec((t, hps * head_v), lambda b, h: (row_block0 + b, v_off + h)),
        pl.BlockSpec((t, hps * head_v), lambda b, h: (row_block0 + b, v_off + nhb + h)),
        pl.BlockSpec((t, hps * head_k), lambda b, h: (row_block0 + b, h)),
        pl.BlockSpec((t, hps * head_k), lambda b, h: (row_block0 + b, nhb + h)),
    ]
    args = [p, p, p, p, p, p, g, g]
    if s0 is not None:
        in_specs.append(state_blk)
        args.append(s0)
    in_specs += [
        pl.BlockSpec(t_f.shape, lambda b, h: (0, 0)),
        pl.BlockSpec(t_b.shape, lambda b, h: (0, 0)),
        pl.BlockSpec(m_f.shape, lambda b, h: (0, 0, 0)),
        pl.BlockSpec(m_b.shape, lambda b, h: (0, 0, 0)),
        pl.BlockSpec((1, head_v), lambda b, h: (0, 0)),
    ]
    args += [t_f, t_b, m_f, m_b, gnorm]
    out_specs = [pl.BlockSpec((t, hps * head_v), lambda b, h: (b, h))]
    out_shape = [jax.ShapeDtypeStruct((nb * t, heads * head_v), BF16)]
    if want_state:
        out_specs.append(state_blk)
        out_shape.append(jax.ShapeDtypeStruct((nb, 1, 2, heads, head_k, head_v), F32))
    return pl.pallas_call(
        functools.partial(_gla_scan_kernel, has_s0=s0 is not None, has_sout=want_state, head_k=head_k,
                          head_v=head_v),
        grid=(nb, nhb),
        in_specs=in_specs,
        out_specs=out_specs,
        out_shape=out_shape,
        scratch_shapes=[
            pltpu.VMEM((2, hps, head_k, head_v), F32),
            pltpu.VMEM((t, hps * head_v), F32),
            pltpu.VMEM((t, hps * head_v), F32),
        ],
        compiler_params=_params(
            "arbitrary", "arbitrary",
            vmem_bytes=2 * t * hps * (4 * head_k * 2 + 3 * head_v * 2 + 2 * head_k * 4)
            + (2 * (int(s0 is not None) + int(want_state)) + 1) * 2 * hps * head_k * head_v * 4
            + 2 * t * hps * head_v * 4 + 8 * MIB),
        name="gla_scan",
    )(*args)


def kernel(x_prompt, x_sample, state_gla, c, c_ctx, ada_w, ada_b, norm_pre, norm_post, ffn_w_gate_up,
           ffn_w_down, fourier_w, gla_w_in, gla_w_gate_up, gla_b_gate, gla_norm, gla_w_out):
    batch, seq, d = x_prompt.shape
    dec_batch, dec_seq, _ = x_sample.shape
    depth = ada_w.shape[0]
    heads, head_k, head_v = state_gla.shape[3:]
    dk_tot = heads * head_k
    dv_tot = heads * head_v
    rank = gla_w_gate_up.shape[2]
    d_ff = ffn_w_down.shape[2]
    ctx_rows = batch * seq
    assert ctx_rows % dec_seq == 0 and dec_seq % seq == 0 and 1 + dec_batch <= SUBLANES
    n_ctx_groups = ctx_rows // dec_seq
    grp = dict(group_rows=dec_seq, n_ctx_groups=n_ctx_groups)
    tm_small = min(dec_seq, 512)
    tm_big = min(dec_seq, 1024)
    tf = 512 if d_ff % 512 == 0 else d_ff

    cond8 = jnp.zeros((SUBLANES, d), F32).at[0].set(c_ctx).at[1:1 + dec_batch].set(c)
    mods = _modulation(cond8, ada_w, ada_b)[:, :1 + dec_batch].reshape(depth, 1 + dec_batch, N_MOD, d)

    n_main = 4 * dk_tot + 2 * dv_tot
    w_in_t = jnp.swapaxes(gla_w_in, 1, 2)

    xs = (x_prompt.reshape(ctx_rows, d), x_sample.reshape(dec_batch * dec_seq, d))
    states = []
    for l in range(depth):
        mod = mods[l]
        j = l // 2
        npre = lambda s: norm_pre[l, s][None, :]
        npost = lambda s: norm_post[l, s][None, :]
        ffn = functools.partial(_ffn, mod=mod, wgu=ffn_w_gate_up, wd=ffn_w_down, layer=l, tf=tf, tm=tm_big,
                                **grp)
        (x,) = ffn(xs, gpre=npre(0), gpost=npost(0), slot=0, k=0, split_out=False)
        if l % 2 == 0:
            f_ctx = _fourier(x, mod, npre(1), t=seq, nb=batch, row_block0=0, mod0=lambda b: 0, k=1)
            f_lat = _fourier(x, mod, npre(1), t=dec_seq, nb=dec_batch, row_block0=n_ctx_groups,
                             mod0=lambda b: 1 + b, k=1)
            x = _dense_post((f_ctx, f_lat), fourier_w, x, mod, npost(1), widx=j, k=1, tm=tm_small, **grp)
        else:
            zero = jnp.zeros((rank, dk_tot), F32)
            wz = jnp.concatenate([jnp.concatenate([gla_w_gate_up[j, 0], zero], axis=1),
                                  jnp.concatenate([zero, gla_w_gate_up[j, 1]], axis=1)], axis=0).astype(BF16)
            bz = gla_b_gate[j].reshape(1, 2 * dk_tot)
            colscale = jnp.ones((n_main,), F32).at[:dk_tot].set(head_k ** -0.5)
            colscale = colscale.at[2 * dk_tot:3 * dk_tot].set(head_k ** -0.5).reshape(1, n_main)
            p, g = _gla_in(x, mod, npre(1), w_in_t, wz, bz, colscale, widx=j, n_main=n_main, k=1,
                           tm=tm_big, n_steps=8, head_k=head_k, dk_tot=dk_tot, **grp)
            gn = gla_norm[j][None, :]
            scan = functools.partial(_gla_scan, p, g, gn, heads=heads, head_k=head_k, head_v=head_v)
            o_ctx, st = scan(None, t=seq, nb=batch, row_block0=0, hps=heads, want_state=True)
            (o_lat,) = scan(state_gla[:, j:j + 1], t=dec_seq, nb=dec_batch, row_block0=n_ctx_groups,
                            hps=min(heads, max(1, 2 * heads * seq // dec_seq)), want_state=False)
            states.append(st)
            x = _dense_post((o_ctx, o_lat), gla_w_out, x, mod, npost(1), widx=j, k=1, tm=tm_small, **grp)
        xs = tuple(ffn((x,), gpre=npre(2), gpost=npost(2), slot=1, k=2, split_out=l == depth - 1))

    y_prompt = xs[0].reshape(batch, seq, d)
    y_sample = xs[1].reshape(dec_batch, dec_seq, d)
    new_state = jnp.concatenate(states, axis=1)
    return (y_prompt, y_sample, new_state)
```

```python
import functools

import numpy as np
import jax
import jax.numpy as jnp
from jax import lax
from jax.experimental import pallas as pl
from jax.experimental.pallas import tpu as pltpu

F32 = jnp.float32
BF16 = jnp.bfloat16

EPS = 1e-6
N_MOD = 9
GRID_W = 64
FOURIER_GROUPS = 4
GATE_TAU = 16.0
CHUNK = 64
ROPE_BASE = 10000.0
LOG2_E = 1.4426950408889634
N_LEVELS = 6
assert CHUNK == 1 << N_LEVELS

V7X_VMEM_BYTES = 64 * 1024 * 1024
MIB = 1024 * 1024
LANES = 128
SUBLANES = 8


def _params(*sem, vmem_bytes):
    request = (int(vmem_bytes) // MIB + 2) * MIB
    assert request < V7X_VMEM_BYTES, request
    return pltpu.CompilerParams(dimension_semantics=sem, vmem_limit_bytes=request)


def _dot(a, b):
    return jnp.dot(a, b, preferred_element_type=F32)


def _dot_nt(a, b):
    return lax.dot_general(a, b, (((1,), (1,)), ((), ())), preferred_element_type=F32)


def _dot_tn(a, b):
    return lax.dot_general(a, b, (((0,), (0,)), ((), ())), preferred_element_type=F32)


def _rms(x):
    return x * lax.rsqrt(jnp.mean(x * x, axis=-1, keepdims=True) + EPS)


def _pre_mod(x, gpre, mod_ref, k):
    gs = gpre * (1.0 + mod_ref[3 * k + 1:3 * k + 2, :])
    return _rms(x) * gs + mod_ref[3 * k:3 * k + 1, :]


def _post_add(x, out, gpost, mod_ref, k, res_w):
    gg = (res_w * mod_ref[3 * k + 2:3 * k + 3, :]) * gpost
    return x + _rms(out) * gg


ROW_BLOCK = 128


def _for_row_blocks(n_rows, fn, row_block=ROW_BLOCK):
    block = min(row_block, n_rows)

    def body(r, carry):
        fn(pl.ds(pl.multiple_of(r * block, block), block))
        return carry

    lax.fori_loop(0, n_rows // block, body, 0)


def _mod_index(i, tm, group_rows, n_ctx_groups):
    return jnp.maximum(i * tm // group_rows - (n_ctx_groups - 1), 0)


def _on_tile_ref(refs, i, n_ctx_tiles, fn):
    if len(refs) == 1:
        fn(refs[0])
    else:
        pl.when(i < n_ctx_tiles)(lambda: fn(refs[0]))
        pl.when(i >= n_ctx_tiles)(lambda: fn(refs[1]))


def _split_row_specs(arrays, tm, width, n_ctx_tiles, ngrid, buffers=None):
    kw = {} if buffers is None else dict(pipeline_mode=pl.Buffered(buffers))
    if ngrid == 1:
        wrap = lambda f: (lambda i: f(i))
    else:
        wrap = lambda f: (lambda i, j: f(i))
    if len(arrays) == 1:
        return [pl.BlockSpec((tm, width), wrap(lambda i: (i, 0)), **kw)]
    return [
        pl.BlockSpec((tm, width), wrap(lambda i: (jnp.minimum(i, n_ctx_tiles - 1), 0)), **kw),
        pl.BlockSpec((tm, width), wrap(lambda i: (jnp.maximum(i - n_ctx_tiles, 0), 0)), **kw),
    ]


def _mod_kernel(c_ref, w_ref, b_ref, o_ref):
    c = c_ref[...]
    s = (c * jax.nn.sigmoid(c)).astype(BF16)
    o_ref[...] = _dot(s, w_ref[...].astype(BF16)) + b_ref[...]


def _modulation(cond8, ada_w, ada_b):
    depth, d, n = ada_w.shape
    tn = min(n, 1024)
    return pl.pallas_call(
        _mod_kernel,
        grid=(depth, n // tn),
        in_specs=[
            pl.BlockSpec((SUBLANES, d), lambda l, j: (0, 0)),
            pl.BlockSpec((None, d, tn), lambda l, j: (l, 0, j)),
            pl.BlockSpec((None, 1, tn), lambda l, j: (l, 0, j)),
        ],
        out_specs=pl.BlockSpec((None, SUBLANES, tn), lambda l, j: (l, 0, j)),
        out_shape=jax.ShapeDtypeStruct((depth, SUBLANES, n), F32),
        compiler_params=_params("arbitrary", "arbitrary", vmem_bytes=d * tn * (2 * 4 + 2) + 2 * MIB),
        name="modulation",
    )(cond8, ada_w, ada_b.reshape(depth, 1, n))


def _on_tile_part(parts, tile, n_ctx_tiles, fn):
    if len(parts) == 1:
        fn(parts[0], tile)
    else:
        pl.when(tile < n_ctx_tiles)(lambda: fn(parts[0], tile))
        pl.when(tile >= n_ctx_tiles)(lambda: fn(parts[1], tile - n_ctx_tiles))


def _ffn_kernel(*refs, n_x, n_out, n_ctx_tiles, k, res_w):
    x_hbm = refs[:n_x]
    mod_ref, gpre_ref, gpost_ref, wg_ref, wu_ref, wd_ref = refs[n_x:n_x + 6]
    o_hbm = refs[n_x + 6:n_x + 6 + n_out]
    h_ref, x_buf, acc_ref, out_sem, x_sem = refs[n_x + 6 + n_out:]
    i = pl.program_id(0)
    j = pl.program_id(1)
    tm = x_buf.shape[0]
    n_blk = x_sem.shape[0]
    blk = tm // n_blk

    def out_copy(ref, tile):
        rows = pl.ds(pl.multiple_of(tile * tm, tm), tm)
        return pltpu.make_async_copy(acc_ref, ref.at[rows, :], out_sem)

    def x_copy(ref, tile, b):
        src = pl.ds(pl.multiple_of(tile * tm + b * blk, blk), blk)
        dst = pl.ds(pl.multiple_of(b * blk, blk), blk)
        return pltpu.make_async_copy(ref.at[src, :], x_buf.at[dst, :], x_sem.at[b])

    @pl.when(j == 0)
    def _():
        def fetch_and_prologue(ref, t):
            for b in range(n_blk):
                x_copy(ref, t, b).start()

            def body(b, carry):
                x_copy(ref, t, b).wait()
                rows = pl.ds(pl.multiple_of(b * blk, blk), blk)
                h_ref[rows, :] = _pre_mod(x_buf[rows, :], gpre_ref[...], mod_ref, k).astype(BF16)
                return carry

            lax.fori_loop(0, n_blk, body, 0)

        _on_tile_part(x_hbm, i, n_ctx_tiles, fetch_and_prologue)

        @pl.when(i > 0)
        def _():
            _on_tile_part(o_hbm, i - 1, n_ctx_tiles, lambda ref, t: out_copy(ref, t).wait())

        acc_ref[...] = jnp.zeros_like(acc_ref)

    h = h_ref[...]
    half = wg_ref.shape[1] // 2
    acts = []
    for c0 in (0, half):
        g = _dot(h, wg_ref[:, c0:c0 + half].astype(BF16))
        u = _dot(h, wu_ref[:, c0:c0 + half].astype(BF16))
        acts.append((g * jax.nn.sigmoid(g) * u).astype(BF16))
    acc_ref[...] += _dot(jnp.concatenate(acts, axis=1), wd_ref[...].astype(BF16))

    @pl.when(j == pl.num_programs(1) - 1)
    def _():
        def epilogue(rows):
            acc_ref[rows, :] = _post_add(x_buf[rows, :], acc_ref[rows, :], gpost_ref[...], mod_ref, k, res_w)

        _for_row_blocks(tm, epilogue)
        _on_tile_part(o_hbm, i, n_ctx_tiles, lambda ref, t: out_copy(ref, t).start())

        @pl.when(i == pl.num_programs(0) - 1)
        def _():
            _on_tile_part(o_hbm, i, n_ctx_tiles, lambda ref, t: out_copy(ref, t).wait())


def _ffn(xs, mod, gpre, gpost, wgu, wd, *, layer, slot, k, tm, tf, split_out, group_rows, n_ctx_groups):
    m = sum(x.shape[0] for x in xs)
    d = xs[0].shape[1]
    nf = wd.shape[2] // tf
    n_ctx_tiles = n_ctx_groups * group_rows // tm
    out_rows = (n_ctx_tiles * tm, m - n_ctx_tiles * tm) if split_out else (m,)
    midx = functools.partial(_mod_index, tm=tm, group_rows=group_rows, n_ctx_groups=n_ctx_groups)
    hbm = pl.BlockSpec(memory_space=pl.ANY)
    return pl.pallas_call(
        functools.partial(_ffn_kernel, n_x=len(xs), n_out=len(out_rows), n_ctx_tiles=n_ctx_tiles, k=k,
                          res_w=0.5),
        grid=(m // tm, nf),
        in_specs=[hbm] * len(xs) + [
            pl.BlockSpec((None, N_MOD, d), lambda i, j: (midx(i), 0, 0)),
            pl.BlockSpec((1, d), lambda i, j: (0, 0)),
            pl.BlockSpec((1, d), lambda i, j: (0, 0)),
            pl.BlockSpec((None, None, d, tf), lambda i, j: (layer, slot, 0, j)),
            pl.BlockSpec((None, None, d, tf), lambda i, j: (layer, slot, 0, j + nf)),
            pl.BlockSpec((None, None, tf, d), lambda i, j: (layer, slot, j, 0)),
        ],
        out_specs=[hbm] * len(out_rows),
        out_shape=[jax.ShapeDtypeStruct((r, d), F32) for r in out_rows],
        scratch_shapes=[
            pltpu.VMEM((tm, d), BF16),
            pltpu.VMEM((tm, d), F32),
            pltpu.VMEM((tm, d), F32),
            pltpu.SemaphoreType.DMA(()),
            pltpu.SemaphoreType.DMA((tm // min(ROW_BLOCK, tm),)),
        ],
        compiler_params=_params("arbitrary", "arbitrary",
                                vmem_bytes=tm * d * (2 + 4 + 4) + 2 * 3 * d * tf * 4
                                + (tf // 2) * (tm * (4 + 4 + 2) + 3 * d * 2) + 2 * MIB),
        name="ffn",
    )(*xs, mod, gpre, gpost, wgu, wgu, wd)


DENSE_SUB_ROWS = 512


def _dense_post_kernel(*refs, n_a, n_ctx_tiles, k, res_w):
    a_refs = refs[:n_a]
    w_ref, x_ref, mod_ref, gpost_ref, o_ref, wbf_ref = refs[n_a:]

    @pl.when(pl.program_id(0) == 0)
    def _():
        wbf_ref[...] = w_ref[...].astype(BF16)

    def body(a_ref):
        tm = o_ref.shape[0]
        sub = min(tm, DENSE_SUB_ROWS)
        for r in range(tm // sub):
            rows = slice(r * sub, (r + 1) * sub)
            y = _dot(a_ref[rows, :], wbf_ref[...])
            o_ref[rows, :] = _post_add(x_ref[rows, :], y, gpost_ref[...], mod_ref, k, res_w)

    _on_tile_ref(a_refs, pl.program_id(0), n_ctx_tiles, body)


def _dense_post(a_parts, w, x, mod, gpost, *, widx, k, tm, group_rows, n_ctx_groups):
    m, d = x.shape
    kk = a_parts[0].shape[1]
    n_ctx_tiles = n_ctx_groups * group_rows // tm
    midx = functools.partial(_mod_index, tm=tm, group_rows=group_rows, n_ctx_groups=n_ctx_groups)
    return pl.pallas_call(
        functools.partial(_dense_post_kernel, n_a=len(a_parts), n_ctx_tiles=n_ctx_tiles, k=k, res_w=1.0),
        grid=(m // tm,),
        in_specs=_split_row_specs(a_parts, tm, kk, n_ctx_tiles, 1) + [
            pl.BlockSpec((None, kk, d), lambda i: (widx, 0, 0), pipeline_mode=pl.Buffered(1)),
            pl.BlockSpec((tm, d), lambda i: (i, 0)),
            pl.BlockSpec((None, N_MOD, d), lambda i: (midx(i), 0, 0)),
            pl.BlockSpec((1, d), lambda i: (0, 0)),
        ],
        out_specs=pl.BlockSpec((tm, d), lambda i: (i, 0)),
        out_shape=jax.ShapeDtypeStruct((m, d), F32),
        scratch_shapes=[pltpu.VMEM((kk, d), BF16)],
        compiler_params=_params("arbitrary", vmem_bytes=2 * (len(a_parts) * tm * kk * 2 + 2 * tm * d * 4)
                                + kk * d * (4 + 2) + 2 * min(tm, DENSE_SUB_ROWS) * d * 4 + 2 * MIB),
        name="dense_post",
    )(*a_parts, w, x, mod, gpost)


def _dft_tables(t, w):
    def cs(n):
        kn = np.outer(np.arange(n), np.arange(n)) % n
        ang = 2.0 * np.pi * kn.astype(np.float64) / n
        return np.cos(ang) / np.sqrt(n), np.sin(ang) / np.sqrt(n)

    ct, st = cs(t)
    cw, sw = cs(w)
    as_bf16 = lambda a: jnp.asarray(a, dtype=F32).astype(BF16)
    return as_bf16(np.concatenate([ct, st], axis=0)), as_bf16(cw), as_bf16(-sw)


def _fourier_kernel(x_ref, mod_ref, gpre_ref, cst_ref, cw_ref, swn_ref, o_ref, *, k):
    t = x_ref.shape[0]
    w = cw_ref.shape[0]
    h = _pre_mod(x_ref[...], gpre_ref[...], mod_ref, k).astype(BF16)
    cst = cst_ref[...]
    for g in range(FOURIER_GROUPS):
        p = _dot(cst, h[:, g * w:(g + 1) * w]).astype(BF16)
        f = _dot(p[:t], cw_ref[...]) + _dot(p[t:], swn_ref[...])
        o_ref[:, g * w:(g + 1) * w] = f.astype(BF16)


def _fourier(x, mod, gpre, *, t, nb, row_block0, mod0, k):
    d = x.shape[1]
    w = d // FOURIER_GROUPS
    cst, cw, swn = _dft_tables(t, w)
    return pl.pallas_call(
        functools.partial(_fourier_kernel, k=k),
        grid=(nb,),
        in_specs=[
            pl.BlockSpec((t, d), lambda b: (row_block0 + b, 0)),
            pl.BlockSpec((None, N_MOD, d), lambda b: (mod0(b), 0, 0)),
            pl.BlockSpec((1, d), lambda b: (0, 0)),
            pl.BlockSpec((2 * t, t), lambda b: (0, 0)),
            pl.BlockSpec((w, w), lambda b: (0, 0)),
            pl.BlockSpec((w, w), lambda b: (0, 0)),
        ],
        out_specs=pl.BlockSpec((t, d), lambda b: (b, 0)),
        out_shape=jax.ShapeDtypeStruct((nb * t, d), BF16),
        compiler_params=_params("arbitrary", vmem_bytes=2 * (t * d * 4 + 2 * t * t * 2 + 2 * w * w * 2
                                                             + t * d * 2) + t * d * 2 + 2 * t * w * 8),
        name="fourier",
    )(x, mod, gpre, cst, cw, swn)


def _rope_tables(t, head_k):
    rows = t // GRID_W
    pairs = head_k // 4
    row = np.repeat(np.arange(rows), GRID_W).astype(np.float64)
    col = np.tile(np.arange(GRID_W), rows).astype(np.float64)
    inv = ROPE_BASE ** (-np.arange(pairs, dtype=np.float64) / pairs)
    ang = np.concatenate([row[:, None] * inv, col[:, None] * inv], axis=-1)
    cos, sin = np.cos(ang), np.sin(ang)
    ce = np.repeat(cos, 2, axis=-1)
    se = np.stack([-sin, sin], axis=-1).reshape(t, head_k)
    return jnp.asarray(ce, dtype=F32), jnp.asarray(se, dtype=F32)


def _gla_in_kernel(x_ref, mod_ref, gpre_ref, w_ref, wlr_ref, wz_ref, bz_ref, cs_ref, ce_ref, se_ref,
                   p_ref, g_ref, h_ref, lr_ref, *, k, n_ctx_tiles, n_qk_steps, head_k):
    i = pl.program_id(0)
    j = pl.program_id(1)

    @pl.when(j == 0)
    def _():
        def prologue(rows):
            h = _pre_mod(x_ref[rows, :], gpre_ref[...], mod_ref, k).astype(BF16)
            h_ref[rows, :] = h
            lr_ref[rows, :] = _dot_nt(h, wlr_ref[...].astype(BF16)).astype(BF16)

        _for_row_blocks(x_ref.shape[0], prologue, 2 * ROW_BLOCK)

    def step(rope):
        z = _dot(lr_ref[...], wz_ref[...]) + bz_ref[...]
        g_ref[...] = (jnp.minimum(z, 0.0) - jnp.log1p(jnp.exp(-jnp.abs(z)))) * (LOG2_E / GATE_TAU)
        h = h_ref[...]
        for s in range(w_ref.shape[0] // head_k):
            cols = slice(s * head_k, (s + 1) * head_k)
            acc = _dot_nt(h, w_ref[cols, :].astype(BF16)) * cs_ref[:, cols]
            if rope:
                lane = lax.broadcasted_iota(jnp.int32, acc.shape, 1)
                partner = jnp.where(lane % 2 == 0, pltpu.roll(acc, head_k - 1, 1), pltpu.roll(acc, 1, 1))
                acc = acc * ce_ref[...] + partner * se_ref[...]
            p_ref[:, cols] = acc.astype(BF16)

    rope = jnp.logical_and(i >= n_ctx_tiles, j < n_qk_steps)
    pl.when(rope)(lambda: step(True))
    pl.when(jnp.logical_not(rope))(lambda: step(False))


def _gla_in(x, mod, gpre, w_in_t, wz, bz, colscale, *, widx, n_main, k, tm, n_steps, head_k, dk_tot,
            group_rows, n_ctx_groups):
    m, d = x.shape
    n = n_main
    tn = n // n_steps
    gcols = 2 * dk_tot // n_steps
    n_lr = w_in_t.shape[1] - n_main
    assert (4 * dk_tot) % tn == 0 and tn % head_k == 0 and gcols % LANES == 0 and n_main % n_lr == 0
    n_ctx_tiles = n_ctx_groups * group_rows // tm
    tiles_per_group = group_rows // tm
    ce, se = _rope_tables(group_rows, head_k)
    midx = functools.partial(_mod_index, tm=tm, group_rows=group_rows, n_ctx_groups=n_ctx_groups)

    def tab_idx(i, j):
        return (jnp.where(i >= n_ctx_tiles, (i - n_ctx_tiles) % tiles_per_group, 0), 0)

    return pl.pallas_call(
        functools.partial(_gla_in_kernel, k=k, n_ctx_tiles=n_ctx_tiles, n_qk_steps=4 * dk_tot // tn,
                          head_k=head_k),
        grid=(m // tm, n_steps),
        in_specs=[
            pl.BlockSpec((tm, d), lambda i, j: (i, 0)),
            pl.BlockSpec((None, N_MOD, d), lambda i, j: (midx(i), 0, 0)),
            pl.BlockSpec((1, d), lambda i, j: (0, 0)),
            pl.BlockSpec((None, tn, d), lambda i, j: (widx, j, 0)),
            pl.BlockSpec((None, n_lr, d), lambda i, j: (widx, n_main // n_lr, 0)),
            pl.BlockSpec((wz.shape[0], gcols), lambda i, j: (0, j)),
            pl.BlockSpec((1, gcols), lambda i, j: (0, j)),
            pl.BlockSpec((1, tn), lambda i, j: (0, j)),
            pl.BlockSpec((tm, head_k), tab_idx),
            pl.BlockSpec((tm, head_k), tab_idx),
        ],
        out_specs=[
            pl.BlockSpec((tm, tn), lambda i, j: (i, j)),
            pl.BlockSpec((tm, gcols), lambda i, j: (i, j)),
        ],
        out_shape=[
            jax.ShapeDtypeStruct((m, n), BF16),
            jax.ShapeDtypeStruct((m, 2 * dk_tot), F32),
        ],
        scratch_shapes=[pltpu.VMEM((tm, d), BF16), pltpu.VMEM((tm, n_lr), BF16)],
        compiler_params=_params("arbitrary", "arbitrary",
                                vmem_bytes=2 * (tm * d * 4 + d * tn * 4 + tm * tn * 2 + tm * gcols * 4
                                                + 2 * tm * head_k * 4) + tm * d * 2
                                + 2 * head_k * (tm * 4 + d * 2) + 2 * tm * gcols * 4 + 2 * MIB),
        name="gla_in",
    )(x, mod, gpre, w_in_t, w_in_t, wz, bz, colscale, ce, se)


def _scan_constants():
    c = CHUNK
    idx = np.arange(c)
    tri = (idx[None, :] <= idx[:, None]).astype(np.float64)
    masks = [np.eye(c)]
    for lev in range(N_LEVELS):
        h = 1 << lev
        start = idx // (2 * h) * (2 * h)
        upper = (idx - start) >= h
        same = start[:, None] == start[None, :]
        masks.append((same & upper[:, None] & ~upper[None, :]).astype(np.float64))
    m_f = np.stack(masks, axis=0)
    m_b = m_f[:, ::-1, ::-1]
    t_f = np.concatenate([tri, tri], axis=1)
    t_b = np.concatenate([tri[::-1, ::-1]] * 2, axis=1)
    return (jnp.asarray(t_f, dtype=BF16), jnp.asarray(t_b, dtype=BF16),
            jnp.asarray(m_f, dtype=F32), jnp.asarray(m_b, dtype=F32))


def _level_exponents(b, g, fwd):
    c, kk = b.shape
    tile = (c // SUBLANES, SUBLANES, kk)
    b8 = b.reshape(tile)
    r8 = lax.broadcasted_iota(jnp.int32, tile, 1)
    out = [jnp.where(r8 % 2 == (1 if fwd else 0), g.reshape(tile), 0.0).reshape(c, kk)]
    lo, hi = (1, 5) if fwd else (2, 6)
    mids = [jnp.where(r8 < 4, b8[:, lo:lo + 1, :], b8[:, hi:hi + 1, :]), b8[:, 3:4, :] if fwd else b8[:, 4:5, :]]
    for lev, mid in zip((1, 2), mids):
        h = 1 << lev
        second_half = (r8 % (2 * h)) >= h
        sign = jnp.where(second_half if fwd else jnp.logical_not(second_half), 1.0, -1.0)
        out.append(((b8 - mid) * sign).reshape(c, kk))
    for lev in range(3, N_LEVELS):
        h = 1 << lev
        bb = b.reshape(c // (2 * h), 2 * h, kk)
        first, second = bb[:, :h, :], bb[:, h:, :]
        if fwd:
            mid = bb[:, h - 1:h, :]
            t = jnp.concatenate([mid - first, second - mid], axis=1)
        else:
            mid = bb[:, h:h + 1, :]
            t = jnp.concatenate([first - mid, mid - second], axis=1)
        out.append(t.reshape(c, kk))
    return out


def _scan_chunks(chains):
    c = CHUNK

    st1 = []
    for q_ref, k_ref, g_ref, v_ref, tri_ref, m_ref, s_ref, o_ref, rows, kcols, vcols, fwd in chains:
        g = g_ref[rows, kcols]
        g_hi = g.astype(BF16)
        g_lo = (g - g_hi.astype(F32)).astype(BF16)
        b = _dot(tri_ref[...], jnp.concatenate([g_hi, g_lo], axis=0))
        diag = _dot_nt(q_ref[rows, kcols], k_ref[rows, kcols])
        st1.append((g, b, diag))

    st2 = []
    for (q_ref, k_ref, g_ref, v_ref, tri_ref, m_ref, s_ref, o_ref, rows, kcols, vcols, fwd), (g, b, diag) in zip(
            chains, st1):
        q_bf = q_ref[rows, kcols]
        k_bf = k_ref[rows, kcols]
        scores = m_ref[0] * diag
        for lev, t in enumerate(_level_exponents(b, g, fwd)):
            e = jnp.exp2(t).astype(BF16)
            scores += m_ref[lev + 1] * _dot_nt(q_bf * e, k_bf * e)
        st2.append(scores)

    for (q_ref, k_ref, g_ref, v_ref, tri_ref, m_ref, s_ref, o_ref, rows, kcols, vcols, fwd), (g, b, diag), scores in zip(
            chains, st1, st2):
        qe = q_ref[rows, kcols] * jnp.exp2(b).astype(BF16)
        o_ref[rows, vcols] = _dot(qe, s_ref[...].astype(BF16)) + _dot(scores.astype(BF16), v_ref[rows, vcols])

    for (q_ref, k_ref, g_ref, v_ref, tri_ref, m_ref, s_ref, o_ref, rows, kcols, vcols, fwd), (g, b, diag) in zip(
            chains, st1):
        last = c - 1 if fwd else 0
        b_last = b[last:last + 1]
        kt = k_ref[rows, kcols] * jnp.exp2(b_last - b).astype(BF16)
        e_col = jnp.broadcast_to(jnp.exp2(b_last), (LANES, b.shape[1])).T
        s = s_ref[...]
        decay = jnp.concatenate([e_col] * (s.shape[1] // LANES), axis=1)
        s_ref[...] = decay * s + _dot_tn(kt, v_ref[rows, vcols])


def _gla_scan_kernel(*refs, has_s0, has_sout, head_k, head_v):
    qf_ref, kf_ref, qb_ref, kb_ref, v_ref, r_ref, gf_ref, gb_ref = refs[:8]
    pos = 8
    if has_s0:
        s0_ref = refs[pos]
        pos += 1
    tf_ref, tb_ref, mf_ref, mb_ref, gn_ref = refs[pos:pos + 5]
    pos += 5
    o_ref = refs[pos]
    pos += 1
    if has_sout:
        sout_ref = refs[pos]
        pos += 1
    s_ref, of_ref, ob_ref = refs[pos:pos + 3]
    if has_sout:
        s_ref = sout_ref

    t = v_ref.shape[0]
    n = t // CHUNK
    hps = v_ref.shape[1] // head_v
    if has_s0:
        s_ref[...] = s0_ref[...]
    else:
        s_ref[...] = jnp.zeros_like(s_ref)

    def body(ci, carry):
        rows_f = pl.ds(pl.multiple_of(ci * CHUNK, CHUNK), CHUNK)
        rows_b = pl.ds(pl.multiple_of((n - 1 - ci) * CHUNK, CHUNK), CHUNK)
        chains = []
        for hh in range(hps):
            kcols = slice(hh * head_k, (hh + 1) * head_k)
            vcols = slice(hh * head_v, (hh + 1) * head_v)
            chains.append((qf_ref, kf_ref, gf_ref, v_ref, tf_ref, mf_ref, s_ref.at[0, hh], of_ref, rows_f,
                           kcols, vcols, True))
            chains.append((qb_ref, kb_ref, gb_ref, v_ref, tb_ref, mb_ref, s_ref.at[1, hh], ob_ref, rows_b,
                           kcols, vcols, False))
        _scan_chunks(chains)
        return carry

    lax.fori_loop(0, n, body, 0, unroll=min(n, 4))

    def norm_body(ci, carry):
        rows = pl.ds(pl.multiple_of(ci * CHUNK, CHUNK), CHUNK)
        for hh in range(hps):
            vcols = slice(hh * head_v, (hh + 1) * head_v)
            o = of_ref[rows, vcols] + ob_ref[rows, vcols]
            r = r_ref[rows, vcols].astype(F32)
            o_ref[rows, vcols] = (_rms(o) * gn_ref[...] * (r * jax.nn.sigmoid(r))).astype(BF16)
        return carry

    lax.fori_loop(0, n, norm_body, 0, unroll=min(n, 4))


def _gla_scan(p, g, gnorm, s0, *, t, nb, row_block0, heads, hps, head_k, head_v, want_state):
    assert heads % hps == 0
    nhb = heads // hps
    t_f, t_b, m_f, m_b = _scan_constants()
    qk_blk = lambda off: pl.BlockSpec((t, hps * head_k), lambda b, h: (row_block0 + b, off * nhb + h))
    v_off = 4 * heads * head_k // (hps * head_v)
    state_blk = pl.BlockSpec((None, None, 2, hps, head_k, head_v), lambda b, h: (b, 0, 0, h, 0, 0))
    in_specs = [
        qk_blk(0), qk_blk(1), qk_blk(2), qk_blk(3),
        pl.BlockSpec((t, hps * head_v), lambda b, h: (row_block0 + b, v_off + h)),
        pl.BlockSpec((t, hps * head_v), lambda b, h: (row_block0 + b, v_off + nhb + h)),
        pl.BlockSpec((t, hps * head_k), lambda b, h: (row_block0 + b, h)),
        pl.BlockSpec((t, hps * head_k), lambda b, h: (row_block0 + b, nhb + h)),
    ]
    args = [p, p, p, p, p, p, g, g]
    if s0 is not None:
        in_specs.append(state_blk)
        args.append(s0)
    in_specs += [
        pl.BlockSpec(t_f.shape, lambda b, h: (0, 0)),
        pl.BlockSpec(t_b.shape, lambda b, h: (0, 0)),
        pl.BlockSpec(m_f.shape, lambda b, h: (0, 0, 0)),
        pl.BlockSpec(m_b.shape, lambda b, h: (0, 0, 0)),
        pl.BlockSpec((1, head_v), lambda b, h: (0, 0)),
    ]
    args += [t_f, t_b, m_f, m_b, gnorm]
    out_specs = [pl.BlockSpec((t, hps * head_v), lambda b, h: (b, h))]
    out_shape = [jax.ShapeDtypeStruct((nb * t, heads * head_v), BF16)]
    if want_state:
        out_specs.append(state_blk)
        out_shape.append(jax.ShapeDtypeStruct((nb, 1, 2, heads, head_k, head_v), F32))
    return pl.pallas_call(
        functools.partial(_gla_scan_kernel, has_s0=s0 is not None, has_sout=want_state, head_k=head_k,
                          head_v=head_v),
        grid=(nb, nhb),
        in_specs=in_specs,
        out_specs=out_specs,
        out_shape=out_shape,
        scratch_shapes=[
            pltpu.VMEM((2, hps, head_k, head_v), F32),
            pltpu.VMEM((t, hps * head_v), F32),
            pltpu.VMEM((t, hps * head_v), F32),
        ],
        compiler_params=_params(
            "arbitrary", "arbitrary",
            vmem_bytes=2 * t * hps * (4 * head_k * 2 + 3 * head_v * 2 + 2 * head_k * 4)
            + (2 * (int(s0 is not None) + int(want_state)) + 1) * 2 * hps * head_k * head_v * 4
            + 2 * t * hps * head_v * 4 + 8 * MIB),
        name="gla_scan",
    )(*args)


def kernel(x_prompt, x_sample, state_gla, c, c_ctx, ada_w, ada_b, norm_pre, norm_post, ffn_w_gate_up,
           ffn_w_down, fourier_w, gla_w_in, gla_w_gate_up, gla_b_gate, gla_norm, gla_w_out):
    batch, seq, d = x_prompt.shape
    dec_batch, dec_seq, _ = x_sample.shape
    depth = ada_w.shape[0]
    heads, head_k, head_v = state_gla.shape[3:]
    dk_tot = heads * head_k
    dv_tot = heads * head_v
    rank = gla_w_gate_up.shape[2]
    d_ff = ffn_w_down.shape[2]
    ctx_rows = batch * seq
    assert ctx_rows % dec_seq == 0 and dec_seq % seq == 0 and 1 + dec_batch <= SUBLANES
    n_ctx_groups = ctx_rows // dec_seq
    grp = dict(group_rows=dec_seq, n_ctx_groups=n_ctx_groups)
    tm_small = min(dec_seq, 512)
    tm_big = min(dec_seq, 1024)
    tf = 512 if d_ff % 512 == 0 else d_ff

    cond8 = jnp.zeros((SUBLANES, d), F32).at[0].set(c_ctx).at[1:1 + dec_batch].set(c)
    mods = _modulation(cond8, ada_w, ada_b)[:, :1 + dec_batch].reshape(depth, 1 + dec_batch, N_MOD, d)

    n_main = 4 * dk_tot + 2 * dv_tot
    w_in_t = jnp.swapaxes(gla_w_in, 1, 2)

    xs = (x_prompt.reshape(ctx_rows, d), x_sample.reshape(dec_batch * dec_seq, d))
    states = []
    for l in range(depth):
        mod = mods[l]
        j = l // 2
        npre = lambda s: norm_pre[l, s][None, :]
        npost = lambda s: norm_post[l, s][None, :]
        ffn = functools.partial(_ffn, mod=mod, wgu=ffn_w_gate_up, wd=ffn_w_down, layer=l, tf=tf, tm=tm_big,
                                **grp)
        (x,) = ffn(xs, gpre=npre(0), gpost=npost(0), slot=0, k=0, split_out=False)
        if l % 2 == 0:
            f_ctx = _fourier(x, mod, npre(1), t=seq, nb=batch, row_block0=0, mod0=lambda b: 0, k=1)
            f_lat = _fourier(x, mod, npre(1), t=dec_seq, nb=dec_batch, row_block0=n_ctx_groups,
                             mod0=lambda b: 1 + b, k=1)
            x = _dense_post((f_ctx, f_lat), fourier_w, x, mod, npost(1), widx=j, k=1, tm=tm_small, **grp)
        else:
            zero = jnp.zeros((rank, dk_tot), F32)
            wz = jnp.concatenate([jnp.concatenate([gla_w_gate_up[j, 0], zero], axis=1),
                                  jnp.concatenate([zero, gla_w_gate_up[j, 1]], axis=1)], axis=0).astype(BF16)
            bz = gla_b_gate[j].reshape(1, 2 * dk_tot)
            colscale = jnp.ones((n_main,), F32).at[:dk_tot].set(head_k ** -0.5)
            colscale = colscale.at[2 * dk_tot:3 * dk_tot].set(head_k ** -0.5).reshape(1, n_main)
            p, g = _gla_in(x, mod, npre(1), w_in_t, wz, bz, colscale, widx=j, n_main=n_main, k=1,
                           tm=tm_big, n_steps=8, head_k=head_k, dk_tot=dk_tot, **grp)
            gn = gla_norm[j][None, :]
            scan = functools.partial(_gla_scan, p, g, gn, heads=heads, head_k=head_k, head_v=head_v)
            o_ctx, st = scan(None, t=seq, nb=batch, row_block0=0, hps=heads, want_state=True)
            (o_lat,) = scan(state_gla[:, j:j + 1], t=dec_seq, nb=dec_batch, row_block0=n_ctx_groups,
                            hps=min(heads, max(1, 2 * heads * seq // dec_seq)), want_state=False)
            states.append(st)
            x = _dense_post((o_ctx, o_lat), gla_w_out, x, mod, npost(1), widx=j, k=1, tm=tm_small, **grp)
        xs = tuple(ffn((x,), gpre=npre(2), gpost=npost(2), slot=1, k=2, split_out=l == depth - 1))

    y_prompt = xs[0].reshape(batch, seq, d)
    y_sample = xs[1].reshape(dec_batch, dec_seq, d)
    new_state = jnp.concatenate(states, axis=1)
    return (y_prompt, y_sample, new_state)
```

```python
import functools

import numpy as np
import jax
import jax.numpy as jnp
from jax import lax
from jax.experimental import pallas as pl
from jax.experimental.pallas import tpu as pltpu

F32 = jnp.float32
BF16 = jnp.bfloat16

EPS = 1e-6
N_MOD = 9
GRID_W = 64
FOURIER_GROUPS = 4
GATE_TAU = 16.0
CHUNK = 64
ROPE_BASE = 10000.0
LOG2_E = 1.4426950408889634
N_LEVELS = 6
assert CHUNK == 1 << N_LEVELS

V7X_VMEM_BYTES = 64 * 1024 * 1024
MIB = 1024 * 1024
LANES = 128
SUBLANES = 8


def _params(*sem, vmem_bytes):
    request = (int(vmem_bytes) // MIB + 2) * MIB
    assert request < V7X_VMEM_BYTES, request
    return pltpu.CompilerParams(dimension_semantics=sem, vmem_limit_bytes=request)


def _dot(a, b):
    return jnp.dot(a, b, preferred_element_type=F32)


def _dot_nt(a, b):
    return lax.dot_general(a, b, (((1,), (1,)), ((), ())), preferred_element_type=F32)


def _dot_tn(a, b):
    return lax.dot_general(a, b, (((0,), (0,)), ((), ())), preferred_element_type=F32)


def _rms(x):
    return x * lax.rsqrt(jnp.mean(x * x, axis=-1, keepdims=True) + EPS)


def _pre_mod(x, gpre, mod_ref, k):
    gs = gpre * (1.0 + mod_ref[3 * k + 1:3 * k + 2, :])
    return _rms(x) * gs + mod_ref[3 * k:3 * k + 1, :]


def _post_add(x, out, gpost, mod_ref, k, res_w):
    gg = (res_w * mod_ref[3 * k + 2:3 * k + 3, :]) * gpost
    return x + _rms(out) * gg


ROW_BLOCK = 128


def _for_row_blocks(n_rows, fn, row_block=ROW_BLOCK):
    block = min(row_block, n_rows)

    def body(r, carry):
        fn(pl.ds(pl.multiple_of(r * block, block), block))
        return carry

    lax.fori_loop(0, n_rows // block, body, 0)


def _mod_index(i, tm, group_rows, n_ctx_groups):
    return jnp.maximum(i * tm // group_rows - (n_ctx_groups - 1), 0)


def _on_tile_ref(refs, i, n_ctx_tiles, fn):
    if len(refs) == 1:
        fn(refs[0])
    else:
        pl.when(i < n_ctx_tiles)(lambda: fn(refs[0]))
        pl.when(i >= n_ctx_tiles)(lambda: fn(refs[1]))


def _split_row_specs(arrays, tm, width, n_ctx_tiles):
    if len(arrays) == 1:
        return [pl.BlockSpec((tm, width), lambda i: (i, 0))]
    return [
        pl.BlockSpec((tm, width), lambda i: (jnp.minimum(i, n_ctx_tiles - 1), 0)),
        pl.BlockSpec((tm, width), lambda i: (jnp.maximum(i - n_ctx_tiles, 0), 0)),
    ]


def _mod_kernel(c_ref, w_ref, b_ref, o_ref):
    c = c_ref[...]
    s = (c * jax.nn.sigmoid(c)).astype(BF16)
    o_ref[...] = _dot(s, w_ref[...].astype(BF16)) + b_ref[...]


def _modulation(cond8, ada_w, ada_b):
    depth, d, n = ada_w.shape
    tn = min(n, 1024)
    return pl.pallas_call(
        _mod_kernel,
        grid=(depth, n // tn),
        in_specs=[
            pl.BlockSpec((SUBLANES, d), lambda l, j: (0, 0)),
            pl.BlockSpec((None, d, tn), lambda l, j: (l, 0, j)),
            pl.BlockSpec((None, 1, tn), lambda l, j: (l, 0, j)),
        ],
        out_specs=pl.BlockSpec((None, SUBLANES, tn), lambda l, j: (l, 0, j)),
        out_shape=jax.ShapeDtypeStruct((depth, SUBLANES, n), F32),
        compiler_params=_params("arbitrary", "arbitrary", vmem_bytes=d * tn * (2 * 4 + 2) + 2 * MIB),
        name="modulation",
    )(cond8, ada_w, ada_b.reshape(depth, 1, n))


def _on_tile_part(parts, tile, n_ctx_tiles, fn):
    if len(parts) == 1:
        fn(parts[0], tile)
    else:
        pl.when(tile < n_ctx_tiles)(lambda: fn(parts[0], tile))
        pl.when(tile >= n_ctx_tiles)(lambda: fn(parts[1], tile - n_ctx_tiles))


def _ffn_kernel(*refs, n_x, n_out, n_ctx_tiles, k, res_w):
    x_hbm = refs[:n_x]
    mod_ref, gpre_ref, gpost_ref, wg_ref, wu_ref, wd_ref = refs[n_x:n_x + 6]
    o_hbm = refs[n_x + 6:n_x + 6 + n_out]
    h_ref, x_buf, acc_ref, out_sem, x_sem = refs[n_x + 6 + n_out:]
    i = pl.program_id(0)
    j = pl.program_id(1)
    tm = x_buf.shape[0]
    n_blk = x_sem.shape[0]
    blk = tm // n_blk

    def out_copy(ref, tile):
        rows = pl.ds(pl.multiple_of(tile * tm, tm), tm)
        return pltpu.make_async_copy(acc_ref, ref.at[rows, :], out_sem)

    def x_copy(ref, tile, b):
        src = pl.ds(pl.multiple_of(tile * tm + b * blk, blk), blk)
        dst = pl.ds(pl.multiple_of(b * blk, blk), blk)
        return pltpu.make_async_copy(ref.at[src, :], x_buf.at[dst, :], x_sem.at[b])

    @pl.when(j == 0)
    def _():
        def fetch_and_prologue(ref, t):
            for b in range(n_blk):
                x_copy(ref, t, b).start()

            def body(b, carry):
                x_copy(ref, t, b).wait()
                rows = pl.ds(pl.multiple_of(b * blk, blk), blk)
                h_ref[rows, :] = _pre_mod(x_buf[rows, :], gpre_ref[...], mod_ref, k).astype(BF16)
                return carry

            lax.fori_loop(0, n_blk, body, 0)

        _on_tile_part(x_hbm, i, n_ctx_tiles, fetch_and_prologue)

        @pl.when(i > 0)
        def _():
            _on_tile_part(o_hbm, i - 1, n_ctx_tiles, lambda ref, t: out_copy(ref, t).wait())

        acc_ref[...] = jnp.zeros_like(acc_ref)

    h = h_ref[...]
    half = wg_ref.shape[1] // 2
    acts = []
    for c0 in (0, half):
        g = _dot(h, wg_ref[:, c0:c0 + half].astype(BF16))
        u = _dot(h, wu_ref[:, c0:c0 + half].astype(BF16))
        acts.append((g * jax.nn.sigmoid(g) * u).astype(BF16))
    acc_ref[...] += _dot(jnp.concatenate(acts, axis=1), wd_ref[...].astype(BF16))

    @pl.when(j == pl.num_programs(1) - 1)
    def _():
        def epilogue(rows):
            acc_ref[rows, :] = _post_add(x_buf[rows, :], acc_ref[rows, :], gpost_ref[...], mod_ref, k, res_w)

        _for_row_blocks(tm, epilogue)
        _on_tile_part(o_hbm, i, n_ctx_tiles, lambda ref, t: out_copy(ref, t).start())

        @pl.when(i == pl.num_programs(0) - 1)
        def _():
            _on_tile_part(o_hbm, i, n_ctx_tiles, lambda ref, t: out_copy(ref, t).wait())


def _ffn(xs, mod, gpre, gpost, wgu, wd, *, layer, slot, k, tm, tf, split_out, group_rows, n_ctx_groups):
    m = sum(x.shape[0] for x in xs)
    d = xs[0].shape[1]
    nf = wd.shape[2] // tf
    n_ctx_tiles = n_ctx_groups * group_rows // tm
    out_rows = (n_ctx_tiles * tm, m - n_ctx_tiles * tm) if split_out else (m,)
    midx = functools.partial(_mod_index, tm=tm, group_rows=group_rows, n_ctx_groups=n_ctx_groups)
    hbm = pl.BlockSpec(memory_space=pl.ANY)
    return pl.pallas_call(
        functools.partial(_ffn_kernel, n_x=len(xs), n_out=len(out_rows), n_ctx_tiles=n_ctx_tiles, k=k,
                          res_w=0.5),
        grid=(m // tm, nf),
        in_specs=[hbm] * len(xs) + [
            pl.BlockSpec((None, N_MOD, d), lambda i, j: (midx(i), 0, 0)),
            pl.BlockSpec((1, d), lambda i, j: (0, 0)),
            pl.BlockSpec((1, d), lambda i, j: (0, 0)),
            pl.BlockSpec((None, None, d, tf), lambda i, j: (layer, slot, 0, j)),
            pl.BlockSpec((None, None, d, tf), lambda i, j: (layer, slot, 0, j + nf)),
            pl.BlockSpec((None, None, tf, d), lambda i, j: (layer, slot, j, 0)),
        ],
        out_specs=[hbm] * len(out_rows),
        out_shape=[jax.ShapeDtypeStruct((r, d), F32) for r in out_rows],
        scratch_shapes=[
            pltpu.VMEM((tm, d), BF16),
            pltpu.VMEM((tm, d), F32),
            pltpu.VMEM((tm, d), F32),
            pltpu.SemaphoreType.DMA(()),
            pltpu.SemaphoreType.DMA((tm // min(ROW_BLOCK, tm),)),
        ],
        compiler_params=_params("arbitrary", "arbitrary",
                                vmem_bytes=tm * d * (2 + 4 + 4) + 2 * 3 * d * tf * 4
                                + (tf // 2) * (tm * (4 + 4 + 2) + 3 * d * 2) + 2 * MIB),
        name="ffn",
    )(*xs, mod, gpre, gpost, wgu, wgu, wd)


def _dense_post_kernel(*refs, n_a, n_ctx_tiles, k, res_w):
    a_refs = refs[:n_a]
    w_ref, x_ref, mod_ref, gpost_ref, o_ref, wbf_ref = refs[n_a:]

    @pl.when(pl.program_id(0) == 0)
    def _():
        wbf_ref[...] = w_ref[...].astype(BF16)

    def body(a_ref):
        y = _dot(a_ref[...], wbf_ref[...])
        o_ref[...] = _post_add(x_ref[...], y, gpost_ref[...], mod_ref, k, res_w)

    _on_tile_ref(a_refs, pl.program_id(0), n_ctx_tiles, body)


def _dense_post(a_parts, w, x, mod, gpost, *, widx, k, tm, group_rows, n_ctx_groups):
    m, d = x.shape
    kk = a_parts[0].shape[1]
    n_ctx_tiles = n_ctx_groups * group_rows // tm
    midx = functools.partial(_mod_index, tm=tm, group_rows=group_rows, n_ctx_groups=n_ctx_groups)
    return pl.pallas_call(
        functools.partial(_dense_post_kernel, n_a=len(a_parts), n_ctx_tiles=n_ctx_tiles, k=k, res_w=1.0),
        grid=(m // tm,),
        in_specs=_split_row_specs(a_parts, tm, kk, n_ctx_tiles) + [
            pl.BlockSpec((None, kk, d), lambda i: (widx, 0, 0), pipeline_mode=pl.Buffered(1)),
            pl.BlockSpec((tm, d), lambda i: (i, 0)),
            pl.BlockSpec((None, N_MOD, d), lambda i: (midx(i), 0, 0)),
            pl.BlockSpec((1, d), lambda i: (0, 0)),
        ],
        out_specs=pl.BlockSpec((tm, d), lambda i: (i, 0)),
        out_shape=jax.ShapeDtypeStruct((m, d), F32),
        scratch_shapes=[pltpu.VMEM((kk, d), BF16)],
        compiler_params=_params("arbitrary", vmem_bytes=2 * (len(a_parts) * tm * kk * 2 + 2 * tm * d * 4)
                                + kk * d * (4 + 2) + 2 * tm * d * 4 + 2 * MIB),
        name="dense_post",
    )(*a_parts, w, x, mod, gpost)


def _dft_tables(t, w):
    def cs(n):
        kn = np.outer(np.arange(n), np.arange(n)) % n
        ang = 2.0 * np.pi * kn.astype(np.float64) / n
        return np.cos(ang) / np.sqrt(n), np.sin(ang) / np.sqrt(n)

    ct, st = cs(t)
    cw, sw = cs(w)
    as_bf16 = lambda a: jnp.asarray(a, dtype=F32).astype(BF16)
    return as_bf16(np.concatenate([ct, st], axis=0)), as_bf16(cw), as_bf16(-sw)


def _fourier_kernel(x_ref, mod_ref, gpre_ref, cst_ref, cw_ref, swn_ref, o_ref, *, k):
    t = x_ref.shape[0]
    w = cw_ref.shape[0]
    h = _pre_mod(x_ref[...], gpre_ref[...], mod_ref, k).astype(BF16)
    cst = cst_ref[...]
    for g in range(FOURIER_GROUPS):
        p = _dot(cst, h[:, g * w:(g + 1) * w]).astype(BF16)
        f = _dot(p[:t], cw_ref[...]) + _dot(p[t:], swn_ref[...])
        o_ref[:, g * w:(g + 1) * w] = f.astype(BF16)


def _fourier(x, mod, gpre, *, t, nb, row_block0, mod0, k):
    d = x.shape[1]
    w = d // FOURIER_GROUPS
    cst, cw, swn = _dft_tables(t, w)
    return pl.pallas_call(
        functools.partial(_fourier_kernel, k=k),
        grid=(nb,),
        in_specs=[
            pl.BlockSpec((t, d), lambda b: (row_block0 + b, 0)),
            pl.BlockSpec((None, N_MOD, d), lambda b: (mod0(b), 0, 0)),
            pl.BlockSpec((1, d), lambda b: (0, 0)),
            pl.BlockSpec((2 * t, t), lambda b: (0, 0)),
            pl.BlockSpec((w, w), lambda b: (0, 0)),
            pl.BlockSpec((w, w), lambda b: (0, 0)),
        ],
        out_specs=pl.BlockSpec((t, d), lambda b: (b, 0)),
        out_shape=jax.ShapeDtypeStruct((nb * t, d), BF16),
        compiler_params=_params("arbitrary", vmem_bytes=2 * (t * d * 4 + 2 * t * t * 2 + 2 * w * w * 2
                                                             + t * d * 2) + t * d * 2 + 2 * t * w * 8),
        name="fourier",
    )(x, mod, gpre, cst, cw, swn)


def _rope_tables(t, head_k):
    rows = t // GRID_W
    pairs = head_k // 4
    row = np.repeat(np.arange(rows), GRID_W).astype(np.float64)
    col = np.tile(np.arange(GRID_W), rows).astype(np.float64)
    inv = ROPE_BASE ** (-np.arange(pairs, dtype=np.float64) / pairs)
    ang = np.concatenate([row[:, None] * inv, col[:, None] * inv], axis=-1)
    cos, sin = np.cos(ang), np.sin(ang)
    ce = np.repeat(cos, 2, axis=-1)
    se = np.stack([-sin, sin], axis=-1).reshape(t, head_k)
    return jnp.asarray(ce, dtype=F32), jnp.asarray(se, dtype=F32)


def _gla_in_kernel(x_ref, mod_ref, gpre_ref, w_ref, wlr_ref, wz_ref, bz_ref, cs_ref, ce_ref, se_ref,
                   p_ref, g_ref, h_ref, lr_ref, *, k, n_ctx_tiles, n_qk_steps, head_k):
    i = pl.program_id(0)
    j = pl.program_id(1)

    @pl.when(j == 0)
    def _():
        def prologue(rows):
            h = _pre_mod(x_ref[rows, :], gpre_ref[...], mod_ref, k).astype(BF16)
            h_ref[rows, :] = h
            lr_ref[rows, :] = _dot_nt(h, wlr_ref[...].astype(BF16)).astype(BF16)

        _for_row_blocks(x_ref.shape[0], prologue, 2 * ROW_BLOCK)

    def step(rope):
        z = _dot(lr_ref[...], wz_ref[...]) + bz_ref[...]
        g_ref[...] = (jnp.minimum(z, 0.0) - jnp.log1p(jnp.exp(-jnp.abs(z)))) * (LOG2_E / GATE_TAU)
        h = h_ref[...]
        for s in range(w_ref.shape[0] // head_k):
            cols = slice(s * head_k, (s + 1) * head_k)
            acc = _dot_nt(h, w_ref[cols, :].astype(BF16)) * cs_ref[:, cols]
            if rope:
                lane = lax.broadcasted_iota(jnp.int32, acc.shape, 1)
                partner = jnp.where(lane % 2 == 0, pltpu.roll(acc, head_k - 1, 1), pltpu.roll(acc, 1, 1))
                acc = acc * ce_ref[...] + partner * se_ref[...]
            p_ref[:, cols] = acc.astype(BF16)

    rope = jnp.logical_and(i >= n_ctx_tiles, j < n_qk_steps)
    pl.when(rope)(lambda: step(True))
    pl.when(jnp.logical_not(rope))(lambda: step(False))


def _gla_in(x, mod, gpre, w_in_t, wz, bz, colscale, *, widx, n_main, k, tm, n_steps, head_k, dk_tot,
            group_rows, n_ctx_groups):
    m, d = x.shape
    n = n_main
    tn = n // n_steps
    gcols = 2 * dk_tot // n_steps
    n_lr = w_in_t.shape[1] - n_main
    assert (4 * dk_tot) % tn == 0 and tn % head_k == 0 and gcols % LANES == 0 and n_main % n_lr == 0
    n_ctx_tiles = n_ctx_groups * group_rows // tm
    tiles_per_group = group_rows // tm
    ce, se = _rope_tables(group_rows, head_k)
    midx = functools.partial(_mod_index, tm=tm, group_rows=group_rows, n_ctx_groups=n_ctx_groups)

    def tab_idx(i, j):
        return (jnp.where(i >= n_ctx_tiles, (i - n_ctx_tiles) % tiles_per_group, 0), 0)

    return pl.pallas_call(
        functools.partial(_gla_in_kernel, k=k, n_ctx_tiles=n_ctx_tiles, n_qk_steps=4 * dk_tot // tn,
                          head_k=head_k),
        grid=(m // tm, n_steps),
        in_specs=[
            pl.BlockSpec((tm, d), lambda i, j: (i, 0)),
            pl.BlockSpec((None, N_MOD, d), lambda i, j: (midx(i), 0, 0)),
            pl.BlockSpec((1, d), lambda i, j: (0, 0)),
            pl.BlockSpec((None, tn, d), lambda i, j: (widx, j, 0)),
            pl.BlockSpec((None, n_lr, d), lambda i, j: (widx, n_main // n_lr, 0)),
            pl.BlockSpec((wz.shape[0], gcols), lambda i, j: (0, j)),
            pl.BlockSpec((1, gcols), lambda i, j: (0, j)),
            pl.BlockSpec((1, tn), lambda i, j: (0, j)),
            pl.BlockSpec((tm, head_k), tab_idx),
            pl.BlockSpec((tm, head_k), tab_idx),
        ],
        out_specs=[
            pl.BlockSpec((tm, tn), lambda i, j: (i, j)),
            pl.BlockSpec((tm, gcols), lambda i, j: (i, j)),
        ],
        out_shape=[
            jax.ShapeDtypeStruct((m, n), BF16),
            jax.ShapeDtypeStruct((m, 2 * dk_tot), F32),
        ],
        scratch_shapes=[pltpu.VMEM((tm, d), BF16), pltpu.VMEM((tm, n_lr), BF16)],
        compiler_params=_params("arbitrary", "arbitrary",
                                vmem_bytes=2 * (tm * d * 4 + d * tn * 4 + tm * tn * 2 + tm * gcols * 4
                                                + 2 * tm * head_k * 4) + tm * d * 2
                                + 2 * head_k * (tm * 4 + d * 2) + 2 * tm * gcols * 4 + 2 * MIB),
        name="gla_in",
    )(x, mod, gpre, w_in_t, w_in_t, wz, bz, colscale, ce, se)


def _scan_constants():
    c = CHUNK
    idx = np.arange(c)
    tri = (idx[None, :] <= idx[:, None]).astype(np.float64)
    masks = [np.eye(c)]
    for lev in range(N_LEVELS):
        h = 1 << lev
        start = idx // (2 * h) * (2 * h)
        upper = (idx - start) >= h
        same = start[:, None] == start[None, :]
        masks.append((same & upper[:, None] & ~upper[None, :]).astype(np.float64))
    m_f = np.stack(masks, axis=0)
    m_b = m_f[:, ::-1, ::-1]
    t_f = np.concatenate([tri, tri], axis=1)
    t_b = np.concatenate([tri[::-1, ::-1]] * 2, axis=1)
    return (jnp.asarray(t_f, dtype=BF16), jnp.asarray(t_b, dtype=BF16),
            jnp.asarray(m_f, dtype=F32), jnp.asarray(m_b, dtype=F32))


def _level_exponents(b, g, fwd):
    c, kk = b.shape
    tile = (c // SUBLANES, SUBLANES, kk)
    b8 = b.reshape(tile)
    r8 = lax.broadcasted_iota(jnp.int32, tile, 1)
    out = [jnp.where(r8 % 2 == (1 if fwd else 0), g.reshape(tile), 0.0).reshape(c, kk)]
    lo, hi = (1, 5) if fwd else (2, 6)
    mids = [jnp.where(r8 < 4, b8[:, lo:lo + 1, :], b8[:, hi:hi + 1, :]), b8[:, 3:4, :] if fwd else b8[:, 4:5, :]]
    for lev, mid in zip((1, 2), mids):
        h = 1 << lev
        second_half = (r8 % (2 * h)) >= h
        sign = jnp.where(second_half if fwd else jnp.logical_not(second_half), 1.0, -1.0)
        out.append(((b8 - mid) * sign).reshape(c, kk))
    for lev in range(3, N_LEVELS):
        h = 1 << lev
        bb = b.reshape(c // (2 * h), 2 * h, kk)
        first, second = bb[:, :h, :], bb[:, h:, :]
        if fwd:
            mid = bb[:, h - 1:h, :]
            t = jnp.concatenate([mid - first, second - mid], axis=1)
        else:
            mid = bb[:, h:h + 1, :]
            t = jnp.concatenate([first - mid, mid - second], axis=1)
        out.append(t.reshape(c, kk))
    return out


def _scan_chunks(chains):
    c = CHUNK

    st1 = []
    for q_ref, k_ref, g_ref, v_ref, tri_ref, m_ref, s_ref, o_ref, rows, kcols, vcols, fwd in chains:
        g = g_ref[rows, kcols]
        g_hi = g.astype(BF16)
        g_lo = (g - g_hi.astype(F32)).astype(BF16)
        b = _dot(tri_ref[...], jnp.concatenate([g_hi, g_lo], axis=0))
        diag = _dot_nt(q_ref[rows, kcols], k_ref[rows, kcols])
        st1.append((g, b, diag))

    st2 = []
    for (q_ref, k_ref, g_ref, v_ref, tri_ref, m_ref, s_ref, o_ref, rows, kcols, vcols, fwd), (g, b, diag) in zip(
            chains, st1):
        q_bf = q_ref[rows, kcols]
        k_bf = k_ref[rows, kcols]
        scores = m_ref[0] * diag
        for lev, t in enumerate(_level_exponents(b, g, fwd)):
            e = jnp.exp2(t).astype(BF16)
            scores += m_ref[lev + 1] * _dot_nt(q_bf * e, k_bf * e)
        st2.append(scores)

    for (q_ref, k_ref, g_ref, v_ref, tri_ref, m_ref, s_ref, o_ref, rows, kcols, vcols, fwd), (g, b, diag), scores in zip(
            chains, st1, st2):
        qe = q_ref[rows, kcols] * jnp.exp2(b).astype(BF16)
        o_ref[rows, vcols] = _dot(qe, s_ref[...].astype(BF16)) + _dot(scores.astype(BF16), v_ref[rows, vcols])

    for (q_ref, k_ref, g_ref, v_ref, tri_ref, m_ref, s_ref, o_ref, rows, kcols, vcols, fwd), (g, b, diag) in zip(
            chains, st1):
        last = c - 1 if fwd else 0
        b_last = b[last:last + 1]
        kt = k_ref[rows, kcols] * jnp.exp2(b_last - b).astype(BF16)
        e_col = jnp.broadcast_to(jnp.exp2(b_last), (LANES, b.shape[1])).T
        s = s_ref[...]
        decay = jnp.concatenate([e_col] * (s.shape[1] // LANES), axis=1)
        s_ref[...] = decay * s + _dot_tn(kt, v_ref[rows, vcols])


def _gla_scan_kernel(*refs, has_s0, has_sout, head_k, head_v):
    qf_ref, kf_ref, qb_ref, kb_ref, v_ref, r_ref, gf_ref, gb_ref = refs[:8]
    pos = 8
    if has_s0:
        s0_ref = refs[pos]
        pos += 1
    tf_ref, tb_ref, mf_ref, mb_ref, gn_ref = refs[pos:pos + 5]
    pos += 5
    o_ref = refs[pos]
    pos += 1
    if has_sout:
        s_ref, of_ref, ob_ref = refs[pos:pos + 3]
    else:
        of_ref, ob_ref, s_ref = refs[pos:pos + 3]

    t = v_ref.shape[0]
    n = t // CHUNK
    hps = v_ref.shape[1] // head_v
    if has_s0:
        s_ref[...] = s0_ref[...]
    else:
        s_ref[...] = jnp.zeros_like(s_ref)

    def body(ci, carry):
        rows_f = pl.ds(pl.multiple_of(ci * CHUNK, CHUNK), CHUNK)
        rows_b = pl.ds(pl.multiple_of((n - 1 - ci) * CHUNK, CHUNK), CHUNK)
        chains = []
        for hh in range(hps):
            kcols = slice(hh * head_k, (hh + 1) * head_k)
            vcols = slice(hh * head_v, (hh + 1) * head_v)
            chains.append((qf_ref, kf_ref, gf_ref, v_ref, tf_ref, mf_ref, s_ref.at[0, hh], of_ref, rows_f,
                           kcols, vcols, True))
            chains.append((qb_ref, kb_ref, gb_ref, v_ref, tb_ref, mb_ref, s_ref.at[1, hh], ob_ref, rows_b,
                           kcols, vcols, False))
        _scan_chunks(chains)
        return carry

    lax.fori_loop(0, n, body, 0, unroll=min(n, 4))

    def norm_body(ci, carry):
        rows = pl.ds(pl.multiple_of(ci * CHUNK, CHUNK), CHUNK)
        for hh in range(hps):
            vcols = slice(hh * head_v, (hh + 1) * head_v)
            o = of_ref[rows, vcols] + ob_ref[rows, vcols]
            r = r_ref[rows, vcols].astype(F32)
            o_ref[rows, vcols] = (_rms(o) * gn_ref[...] * (r * jax.nn.sigmoid(r))).astype(BF16)
        return carry

    lax.fori_loop(0, n, norm_body, 0, unroll=min(n, 4))


def _gla_scan(p, g, gnorm, s0, *, t, nb, row_block0, heads, hps, head_k, head_v, want_state):
    assert heads % hps == 0
    nhb = heads // hps
    t_f, t_b, m_f, m_b = _scan_constants()
    qk_blk = lambda off: pl.BlockSpec((t, hps * head_k), lambda b, h: (row_block0 + b, off * nhb + h))
    v_off = 4 * heads * head_k // (hps * head_v)
    state_blk = pl.BlockSpec((None, None, 2, hps, head_k, head_v), lambda b, h: (b, 0, 0, h, 0, 0))
    in_specs = [
        qk_blk(0), qk_blk(1), qk_blk(2), qk_blk(3),
        pl.BlockSpec((t, hps * head_v), lambda b, h: (row_block0 + b, v_off + h)),
        pl.BlockSpec((t, hps * head_v), lambda b, h: (row_block0 + b, v_off + nhb + h)),
        pl.BlockSpec((t, hps * head_k), lambda b, h: (row_block0 + b, h)),
        pl.BlockSpec((t, hps * head_k), lambda b, h: (row_block0 + b, nhb + h)),
    ]
    args = [p, p, p, p, p, p, g, g]
    if s0 is not None:
        in_specs.append(state_blk)
        args.append(s0)
    in_specs += [
        pl.BlockSpec(t_f.shape, lambda b, h: (0, 0)),
        pl.BlockSpec(t_b.shape, lambda b, h: (0, 0)),
        pl.BlockSpec(m_f.shape, lambda b, h: (0, 0, 0)),
        pl.BlockSpec(m_b.shape, lambda b, h: (0, 0, 0)),
        pl.BlockSpec((1, head_v), lambda b, h: (0, 0)),
    ]
    args += [t_f, t_b, m_f, m_b, gnorm]
    out_specs = [pl.BlockSpec((t, hps * head_v), lambda b, h: (b, h))]
    out_shape = [jax.ShapeDtypeStruct((nb * t, heads * head_v), BF16)]
    if want_state:
        out_specs.append(state_blk)
        out_shape.append(jax.ShapeDtypeStruct((nb, 1, 2, heads, head_k, head_v), F32))
    return pl.pallas_call(
        functools.partial(_gla_scan_kernel, has_s0=s0 is not None, has_sout=want_state, head_k=head_k,
                          head_v=head_v),
        grid=(nb, nhb),
        in_specs=in_specs,
        out_specs=out_specs,
        out_shape=out_shape,
        scratch_shapes=[
            pltpu.VMEM((t, hps * head_v), F32),
            pltpu.VMEM((t, hps * head_v), F32),
        ] + ([] if want_state else [pltpu.VMEM((2, hps, head_k, head_v), F32)]),
        compiler_params=_params(
            "arbitrary", "arbitrary",
            vmem_bytes=2 * t * hps * (4 * head_k * 2 + 3 * head_v * 2 + 2 * head_k * 4)
            + (2 * (int(s0 is not None) + int(want_state)) + int(not want_state)) * 2 * hps * head_k * head_v * 4
            + 2 * t * hps * head_v * 4 + 8 * MIB),
        name="gla_scan",
    )(*args)


def kernel(x_prompt, x_sample, state_gla, c, c_ctx, ada_w, ada_b, norm_pre, norm_post, ffn_w_gate_up,
           ffn_w_down, fourier_w, gla_w_in, gla_w_gate_up, gla_b_gate, gla_norm, gla_w_out):
    batch, seq, d = x_prompt.shape
    dec_batch, dec_seq, _ = x_sample.shape
    depth = ada_w.shape[0]
    heads, head_k, head_v = state_gla.shape[3:]
    dk_tot = heads * head_k
    dv_tot = heads * head_v
    rank = gla_w_gate_up.shape[2]
    d_ff = ffn_w_down.shape[2]
    ctx_rows = batch * seq
    assert ctx_rows % dec_seq == 0 and dec_seq % seq == 0 and 1 + dec_batch <= SUBLANES
    n_ctx_groups = ctx_rows // dec_seq
    grp = dict(group_rows=dec_seq, n_ctx_groups=n_ctx_groups)
    tm_small = min(dec_seq, 512)
    tm_big = min(dec_seq, 1024)
    tf = 512 if d_ff % 512 == 0 else d_ff

    cond8 = jnp.zeros((SUBLANES, d), F32).at[0].set(c_ctx).at[1:1 + dec_batch].set(c)
    mods = _modulation(cond8, ada_w, ada_b)[:, :1 + dec_batch].reshape(depth, 1 + dec_batch, N_MOD, d)

    n_main = 4 * dk_tot + 2 * dv_tot
    w_in_t = jnp.swapaxes(gla_w_in, 1, 2)

    xs = (x_prompt.reshape(ctx_rows, d), x_sample.reshape(dec_batch * dec_seq, d))
    states = []
    for l in range(depth):
        mod = mods[l]
        j = l // 2
        npre = lambda s: norm_pre[l, s][None, :]
        npost = lambda s: norm_post[l, s][None, :]
        ffn = functools.partial(_ffn, mod=mod, wgu=ffn_w_gate_up, wd=ffn_w_down, layer=l, tf=tf, tm=tm_big,
                                **grp)
        (x,) = ffn(xs, gpre=npre(0), gpost=npost(0), slot=0, k=0, split_out=False)
        if l % 2 == 0:
            f_ctx = _fourier(x, mod, npre(1), t=seq, nb=batch, row_block0=0, mod0=lambda b: 0, k=1)
            f_lat = _fourier(x, mod, npre(1), t=dec_seq, nb=dec_batch, row_block0=n_ctx_groups,
                             mod0=lambda b: 1 + b, k=1)
            x = _dense_post((f_ctx, f_lat), fourier_w, x, mod, npost(1), widx=j, k=1, tm=tm_small, **grp)
        else:
            zero = jnp.zeros((rank, dk_tot), F32)
            wz = jnp.concatenate([jnp.concatenate([gla_w_gate_up[j, 0], zero], axis=1),
                                  jnp.concatenate([zero, gla_w_gate_up[j, 1]], axis=1)], axis=0).astype(BF16)
            bz = gla_b_gate[j].reshape(1, 2 * dk_tot)
            colscale = jnp.ones((n_main,), F32).at[:dk_tot].set(head_k ** -0.5)
            colscale = colscale.at[2 * dk_tot:3 * dk_tot].set(head_k ** -0.5).reshape(1, n_main)
            p, g = _gla_in(x, mod, npre(1), w_in_t, wz, bz, colscale, widx=j, n_main=n_main, k=1,
                           tm=tm_big, n_steps=8, head_k=head_k, dk_tot=dk_tot, **grp)
            gn = gla_norm[j][None, :]
            scan = functools.partial(_gla_scan, p, g, gn, heads=heads, head_k=head_k, head_v=head_v)
            o_ctx, st = scan(None, t=seq, nb=batch, row_block0=0, hps=heads, want_state=True)
            (o_lat,) = scan(state_gla[:, j:j + 1], t=dec_seq, nb=dec_batch, row_block0=n_ctx_groups,
                            hps=min(heads, max(1, 2 * heads * seq // dec_seq)), want_state=False)
            states.append(st)
            x = _dense_post((o_ctx, o_lat), gla_w_out, x, mod, npost(1), widx=j, k=1, tm=tm_small, **grp)
        xs = tuple(ffn((x,), gpre=npre(2), gpost=npost(2), slot=1, k=2, split_out=l == depth - 1))

    y_prompt = xs[0].reshape(batch, seq, d)
    y_sample = xs[1].reshape(dec_batch, dec_seq, d)
    new_state = jnp.concatenate(states, axis=1)
    return (y_prompt, y_sample, new_state)
```

```python
import functools

import numpy as np
import jax
import jax.numpy as jnp
from jax import lax
from jax.experimental import pallas as pl
from jax.experimental.pallas import tpu as pltpu

F32 = jnp.float32
BF16 = jnp.bfloat16

EPS = 1e-6
N_MOD = 9
GRID_W = 64
FOURIER_GROUPS = 4
GATE_TAU = 16.0
CHUNK = 64
ROPE_BASE = 10000.0
LOG2_E = 1.4426950408889634
N_LEVELS = 6
assert CHUNK == 1 << N_LEVELS

V7X_VMEM_BYTES = 64 * 1024 * 1024
MIB = 1024 * 1024
LANES = 128
SUBLANES = 8


def _params(*sem, vmem_bytes):
    request = (int(vmem_bytes) // MIB + 2) * MIB
    assert request < V7X_VMEM_BYTES, request
    return pltpu.CompilerParams(dimension_semantics=sem, vmem_limit_bytes=request)


def _dot(a, b):
    return jnp.dot(a, b, preferred_element_type=F32)


def _dot_nt(a, b):
    return lax.dot_general(a, b, (((1,), (1,)), ((), ())), preferred_element_type=F32)


def _dot_tn(a, b):
    return lax.dot_general(a, b, (((0,), (0,)), ((), ())), preferred_element_type=F32)


def _rms(x):
    return x * lax.rsqrt(jnp.mean(x * x, axis=-1, keepdims=True) + EPS)


def _pre_mod(x, gpre, mod_ref, k):
    gs = gpre * (1.0 + mod_ref[3 * k + 1:3 * k + 2, :])
    return _rms(x) * gs + mod_ref[3 * k:3 * k + 1, :]


def _post_add(x, out, gpost, mod_ref, k, res_w):
    gg = (res_w * mod_ref[3 * k + 2:3 * k + 3, :]) * gpost
    return x + _rms(out) * gg


ROW_BLOCK = 128


def _for_row_blocks(n_rows, fn, row_block=ROW_BLOCK):
    block = min(row_block, n_rows)

    def body(r, carry):
        fn(pl.ds(pl.multiple_of(r * block, block), block))
        return carry

    lax.fori_loop(0, n_rows // block, body, 0)


def _mod_index(i, tm, group_rows, n_ctx_groups):
    return jnp.maximum(i * tm // group_rows - (n_ctx_groups - 1), 0)


def _on_tile_ref(refs, i, n_ctx_tiles, fn):
    if len(refs) == 1:
        fn(refs[0])
    else:
        pl.when(i < n_ctx_tiles)(lambda: fn(refs[0]))
        pl.when(i >= n_ctx_tiles)(lambda: fn(refs[1]))


def _split_row_specs(arrays, tm, width, n_ctx_tiles):
    if len(arrays) == 1:
        return [pl.BlockSpec((tm, width), lambda i: (i, 0))]
    return [
        pl.BlockSpec((tm, width), lambda i: (jnp.minimum(i, n_ctx_tiles - 1), 0)),
        pl.BlockSpec((tm, width), lambda i: (jnp.maximum(i - n_ctx_tiles, 0), 0)),
    ]


def _mod_kernel(c_ref, w_ref, b_ref, o_ref):
    c = c_ref[...]
    s = (c * jax.nn.sigmoid(c)).astype(BF16)
    o_ref[...] = _dot(s, w_ref[...].astype(BF16)) + b_ref[...]


def _modulation(cond8, ada_w, ada_b):
    depth, d, n = ada_w.shape
    tn = min(n, 1024)
    return pl.pallas_call(
        _mod_kernel,
        grid=(depth, n // tn),
        in_specs=[
            pl.BlockSpec((SUBLANES, d), lambda l, j: (0, 0)),
            pl.BlockSpec((None, d, tn), lambda l, j: (l, 0, j)),
            pl.BlockSpec((None, 1, tn), lambda l, j: (l, 0, j)),
        ],
        out_specs=pl.BlockSpec((None, SUBLANES, tn), lambda l, j: (l, 0, j)),
        out_shape=jax.ShapeDtypeStruct((depth, SUBLANES, n), F32),
        compiler_params=_params("arbitrary", "arbitrary", vmem_bytes=d * tn * (2 * 4 + 2) + 2 * MIB),
        name="modulation",
    )(cond8, ada_w, ada_b.reshape(depth, 1, n))


def _on_tile_part(parts, tile, n_ctx_tiles, fn):
    if len(parts) == 1:
        fn(parts[0], tile)
    else:
        pl.when(tile < n_ctx_tiles)(lambda: fn(parts[0], tile))
        pl.when(tile >= n_ctx_tiles)(lambda: fn(parts[1], tile - n_ctx_tiles))


def _ffn_kernel(*refs, n_x, n_out, n_ctx_tiles, k, res_w):
    x_hbm = refs[:n_x]
    mod_ref, gpre_ref, gpost_ref, wg_ref, wu_ref, wd_ref = refs[n_x:n_x + 6]
    o_hbm = refs[n_x + 6:n_x + 6 + n_out]
    h_ref, x_buf, acc_ref, out_sem, x_sem = refs[n_x + 6 + n_out:]
    i = pl.program_id(0)
    j = pl.program_id(1)
    tm = x_buf.shape[0]
    n_blk = x_sem.shape[0]
    blk = tm // n_blk

    def out_copy(ref, tile):
        rows = pl.ds(pl.multiple_of(tile * tm, tm), tm)
        return pltpu.make_async_copy(acc_ref, ref.at[rows, :], out_sem)

    def x_copy(ref, tile, b):
        src = pl.ds(pl.multiple_of(tile * tm + b * blk, blk), blk)
        dst = pl.ds(pl.multiple_of(b * blk, blk), blk)
        return pltpu.make_async_copy(ref.at[src, :], x_buf.at[dst, :], x_sem.at[b])

    @pl.when(j == 0)
    def _():
        def fetch_and_prologue(ref, t):
            for b in range(n_blk):
                x_copy(ref, t, b).start()

            def body(b, carry):
                x_copy(ref, t, b).wait()
                rows = pl.ds(pl.multiple_of(b * blk, blk), blk)
                h_ref[rows, :] = _pre_mod(x_buf[rows, :], gpre_ref[...], mod_ref, k).astype(BF16)
                return carry

            lax.fori_loop(0, n_blk, body, 0)

        _on_tile_part(x_hbm, i, n_ctx_tiles, fetch_and_prologue)

        @pl.when(i > 0)
        def _():
            _on_tile_part(o_hbm, i - 1, n_ctx_tiles, lambda ref, t: out_copy(ref, t).wait())

    def hidden_tile(first):
        h = h_ref[...]
        half = wg_ref.shape[1] // 2
        acts = []
        for c0 in (0, half):
            g = _dot(h, wg_ref[:, c0:c0 + half].astype(BF16))
            u = _dot(h, wu_ref[:, c0:c0 + half].astype(BF16))
            acts.append((g * jax.nn.sigmoid(g) * u).astype(BF16))
        y = _dot(jnp.concatenate(acts, axis=1), wd_ref[...].astype(BF16))
        if first:
            acc_ref[...] = y
        else:
            acc_ref[...] += y

    pl.when(j == 0)(lambda: hidden_tile(True))
    pl.when(j > 0)(lambda: hidden_tile(False))

    @pl.when(j == pl.num_programs(1) - 1)
    def _():
        def epilogue(rows):
            acc_ref[rows, :] = _post_add(x_buf[rows, :], acc_ref[rows, :], gpost_ref[...], mod_ref, k, res_w)

        _for_row_blocks(tm, epilogue)
        _on_tile_part(o_hbm, i, n_ctx_tiles, lambda ref, t: out_copy(ref, t).start())

        @pl.when(i == pl.num_programs(0) - 1)
        def _():
            _on_tile_part(o_hbm, i, n_ctx_tiles, lambda ref, t: out_copy(ref, t).wait())


def _ffn(xs, mod, gpre, gpost, wgu, wd, *, layer, slot, k, tm, tf, split_out, group_rows, n_ctx_groups):
    m = sum(x.shape[0] for x in xs)
    d = xs[0].shape[1]
    nf = wd.shape[2] // tf
    n_ctx_tiles = n_ctx_groups * group_rows // tm
    out_rows = (n_ctx_tiles * tm, m - n_ctx_tiles * tm) if split_out else (m,)
    midx = functools.partial(_mod_index, tm=tm, group_rows=group_rows, n_ctx_groups=n_ctx_groups)
    hbm = pl.BlockSpec(memory_space=pl.ANY)
    return pl.pallas_call(
        functools.partial(_ffn_kernel, n_x=len(xs), n_out=len(out_rows), n_ctx_tiles=n_ctx_tiles, k=k,
                          res_w=0.5),
        grid=(m // tm, nf),
        in_specs=[hbm] * len(xs) + [
            pl.BlockSpec((None, N_MOD, d), lambda i, j: (midx(i), 0, 0)),
            pl.BlockSpec((1, d), lambda i, j: (0, 0)),
            pl.BlockSpec((1, d), lambda i, j: (0, 0)),
            pl.BlockSpec((None, None, d, tf), lambda i, j: (layer, slot, 0, j)),
            pl.BlockSpec((None, None, d, tf), lambda i, j: (layer, slot, 0, j + nf)),
            pl.BlockSpec((None, None, tf, d), lambda i, j: (layer, slot, j, 0)),
        ],
        out_specs=[hbm] * len(out_rows),
        out_shape=[jax.ShapeDtypeStruct((r, d), F32) for r in out_rows],
        scratch_shapes=[
            pltpu.VMEM((tm, d), BF16),
            pltpu.VMEM((tm, d), F32),
            pltpu.VMEM((tm, d), F32),
            pltpu.SemaphoreType.DMA(()),
            pltpu.SemaphoreType.DMA((tm // min(ROW_BLOCK, tm),)),
        ],
        compiler_params=_params("arbitrary", "arbitrary",
                                vmem_bytes=tm * d * (2 + 4 + 4) + 2 * 3 * d * tf * 4
                                + (tf // 2) * (tm * (4 + 4 + 2) + 3 * d * 2) + 2 * MIB),
        name="ffn",
    )(*xs, mod, gpre, gpost, wgu, wgu, wd)


def _dense_post_kernel(*refs, n_a, n_ctx_tiles, k, res_w):
    a_refs = refs[:n_a]
    w_ref, x_ref, mod_ref, gpost_ref, o_ref, wbf_ref = refs[n_a:]

    @pl.when(pl.program_id(0) == 0)
    def _():
        wbf_ref[...] = w_ref[...].astype(BF16)

    def body(a_ref):
        y = _dot(a_ref[...], wbf_ref[...])
        o_ref[...] = _post_add(x_ref[...], y, gpost_ref[...], mod_ref, k, res_w)

    _on_tile_ref(a_refs, pl.program_id(0), n_ctx_tiles, body)


def _dense_post(a_parts, w, x, mod, gpost, *, widx, k, tm, group_rows, n_ctx_groups):
    m, d = x.shape
    kk = a_parts[0].shape[1]
    n_ctx_tiles = n_ctx_groups * group_rows // tm
    midx = functools.partial(_mod_index, tm=tm, group_rows=group_rows, n_ctx_groups=n_ctx_groups)
    return pl.pallas_call(
        functools.partial(_dense_post_kernel, n_a=len(a_parts), n_ctx_tiles=n_ctx_tiles, k=k, res_w=1.0),
        grid=(m // tm,),
        in_specs=_split_row_specs(a_parts, tm, kk, n_ctx_tiles) + [
            pl.BlockSpec((None, kk, d), lambda i: (widx, 0, 0), pipeline_mode=pl.Buffered(1)),
            pl.BlockSpec((tm, d), lambda i: (i, 0)),
            pl.BlockSpec((None, N_MOD, d), lambda i: (midx(i), 0, 0)),
            pl.BlockSpec((1, d), lambda i: (0, 0)),
        ],
        out_specs=pl.BlockSpec((tm, d), lambda i: (i, 0)),
        out_shape=jax.ShapeDtypeStruct((m, d), F32),
        scratch_shapes=[pltpu.VMEM((kk, d), BF16)],
        compiler_params=_params("arbitrary", vmem_bytes=2 * (len(a_parts) * tm * kk * 2 + 2 * tm * d * 4)
                                + kk * d * (4 + 2) + 2 * tm * d * 4 + 2 * MIB),
        name="dense_post",
    )(*a_parts, w, x, mod, gpost)


def _dft_tables(t, w):
    def cs(n):
        kn = np.outer(np.arange(n), np.arange(n)) % n
        ang = 2.0 * np.pi * kn.astype(np.float64) / n
        return np.cos(ang) / np.sqrt(n), np.sin(ang) / np.sqrt(n)

    ct, st = cs(t)
    cw, sw = cs(w)
    as_bf16 = lambda a: jnp.asarray(a, dtype=F32).astype(BF16)
    return as_bf16(np.concatenate([ct, st], axis=0)), as_bf16(cw), as_bf16(-sw)


def _fourier_kernel(x_ref, mod_ref, gpre_ref, cst_ref, cw_ref, swn_ref, o_ref, *, k):
    t = x_ref.shape[0]
    w = cw_ref.shape[0]
    h = _pre_mod(x_ref[...], gpre_ref[...], mod_ref, k).astype(BF16)
    cst = cst_ref[...]
    for g in range(FOURIER_GROUPS):
        p = _dot(cst, h[:, g * w:(g + 1) * w]).astype(BF16)
        f = _dot(p[:t], cw_ref[...]) + _dot(p[t:], swn_ref[...])
        o_ref[:, g * w:(g + 1) * w] = f.astype(BF16)


def _fourier(x, mod, gpre, *, t, nb, row_block0, mod0, k):
    d = x.shape[1]
    w = d // FOURIER_GROUPS
    cst, cw, swn = _dft_tables(t, w)
    return pl.pallas_call(
        functools.partial(_fourier_kernel, k=k),
        grid=(nb,),
        in_specs=[
            pl.BlockSpec((t, d), lambda b: (row_block0 + b, 0)),
            pl.BlockSpec((None, N_MOD, d), lambda b: (mod0(b), 0, 0)),
            pl.BlockSpec((1, d), lambda b: (0, 0)),
            pl.BlockSpec((2 * t, t), lambda b: (0, 0)),
            pl.BlockSpec((w, w), lambda b: (0, 0)),
            pl.BlockSpec((w, w), lambda b: (0, 0)),
        ],
        out_specs=pl.BlockSpec((t, d), lambda b: (b, 0)),
        out_shape=jax.ShapeDtypeStruct((nb * t, d), BF16),
        compiler_params=_params("arbitrary", vmem_bytes=2 * (t * d * 4 + 2 * t * t * 2 + 2 * w * w * 2
                                                             + t * d * 2) + t * d * 2 + 2 * t * w * 8),
        name="fourier",
    )(x, mod, gpre, cst, cw, swn)


def _rope_tables(t, head_k):
    rows = t // GRID_W
    pairs = head_k // 4
    row = np.repeat(np.arange(rows), GRID_W).astype(np.float64)
    col = np.tile(np.arange(GRID_W), rows).astype(np.float64)
    inv = ROPE_BASE ** (-np.arange(pairs, dtype=np.float64) / pairs)
    ang = np.concatenate([row[:, None] * inv, col[:, None] * inv], axis=-1)
    cos, sin = np.cos(ang), np.sin(ang)
    ce = np.repeat(cos, 2, axis=-1)
    se = np.stack([-sin, sin], axis=-1).reshape(t, head_k)
    return jnp.asarray(ce, dtype=F32), jnp.asarray(se, dtype=F32)


def _gla_in_kernel(x_ref, mod_ref, gpre_ref, w_ref, wlr_ref, wz_ref, bz_ref, cs_ref, ce_ref, se_ref,
                   p_ref, g_ref, h_ref, lr_ref, *, k, n_ctx_tiles, n_qk_steps, head_k):
    i = pl.program_id(0)
    j = pl.program_id(1)

    @pl.when(j == 0)
    def _():
        def prologue(rows):
            h = _pre_mod(x_ref[rows, :], gpre_ref[...], mod_ref, k).astype(BF16)
            h_ref[rows, :] = h
            lr_ref[rows, :] = _dot_nt(h, wlr_ref[...].astype(BF16)).astype(BF16)

        _for_row_blocks(x_ref.shape[0], prologue, 2 * ROW_BLOCK)

    def step(rope):
        z = _dot(lr_ref[...], wz_ref[...]) + bz_ref[...]
        g_ref[...] = (jnp.minimum(z, 0.0) - jnp.log1p(jnp.exp(-jnp.abs(z)))) * (LOG2_E / GATE_TAU)
        h = h_ref[...]
        for s in range(w_ref.shape[0] // head_k):
            cols = slice(s * head_k, (s + 1) * head_k)
            acc = _dot_nt(h, w_ref[cols, :].astype(BF16)) * cs_ref[:, cols]
            if rope:
                lane = lax.broadcasted_iota(jnp.int32, acc.shape, 1)
                partner = jnp.where(lane % 2 == 0, pltpu.roll(acc, head_k - 1, 1), pltpu.roll(acc, 1, 1))
                acc = acc * ce_ref[...] + partner * se_ref[...]
            p_ref[:, cols] = acc.astype(BF16)

    rope = jnp.logical_and(i >= n_ctx_tiles, j < n_qk_steps)
    pl.when(rope)(lambda: step(True))
    pl.when(jnp.logical_not(rope))(lambda: step(False))


def _gla_in(x, mod, gpre, w_in_t, wz, bz, colscale, *, widx, n_main, k, tm, n_steps, head_k, dk_tot,
            group_rows, n_ctx_groups):
    m, d = x.shape
    n = n_main
    tn = n // n_steps
    gcols = 2 * dk_tot // n_steps
    n_lr = w_in_t.shape[1] - n_main
    assert (4 * dk_tot) % tn == 0 and tn % head_k == 0 and gcols % LANES == 0 and n_main % n_lr == 0
    n_ctx_tiles = n_ctx_groups * group_rows // tm
    tiles_per_group = group_rows // tm
    ce, se = _rope_tables(group_rows, head_k)
    midx = functools.partial(_mod_index, tm=tm, group_rows=group_rows, n_ctx_groups=n_ctx_groups)

    def tab_idx(i, j):
        return (jnp.where(i >= n_ctx_tiles, (i - n_ctx_tiles) % tiles_per_group, 0), 0)

    return pl.pallas_call(
        functools.partial(_gla_in_kernel, k=k, n_ctx_tiles=n_ctx_tiles, n_qk_steps=4 * dk_tot // tn,
                          head_k=head_k),
        grid=(m // tm, n_steps),
        in_specs=[
            pl.BlockSpec((tm, d), lambda i, j: (i, 0)),
            pl.BlockSpec((None, N_MOD, d), lambda i, j: (midx(i), 0, 0)),
            pl.BlockSpec((1, d), lambda i, j: (0, 0)),
            pl.BlockSpec((None, tn, d), lambda i, j: (widx, j, 0)),
            pl.BlockSpec((None, n_lr, d), lambda i, j: (widx, n_main // n_lr, 0)),
            pl.BlockSpec((wz.shape[0], gcols), lambda i, j: (0, j)),
            pl.BlockSpec((1, gcols), lambda i, j: (0, j)),
            pl.BlockSpec((1, tn), lambda i, j: (0, j)),
            pl.BlockSpec((tm, head_k), tab_idx),
            pl.BlockSpec((tm, head_k), tab_idx),
        ],
        out_specs=[
            pl.BlockSpec((tm, tn), lambda i, j: (i, j)),
            pl.BlockSpec((tm, gcols), lambda i, j: (i, j)),
        ],
        out_shape=[
            jax.ShapeDtypeStruct((m, n), BF16),
            jax.ShapeDtypeStruct((m, 2 * dk_tot), F32),
        ],
        scratch_shapes=[pltpu.VMEM((tm, d), BF16), pltpu.VMEM((tm, n_lr), BF16)],
        compiler_params=_params("arbitrary", "arbitrary",
                                vmem_bytes=2 * (tm * d * 4 + d * tn * 4 + tm * tn * 2 + tm * gcols * 4
                                                + 2 * tm * head_k * 4) + tm * d * 2
                                + 2 * head_k * (tm * 4 + d * 2) + 2 * tm * gcols * 4 + 2 * MIB),
        name="gla_in",
    )(x, mod, gpre, w_in_t, w_in_t, wz, bz, colscale, ce, se)


def _scan_constants():
    c = CHUNK
    idx = np.arange(c)
    tri = (idx[None, :] <= idx[:, None]).astype(np.float64)
    masks = [np.eye(c)]
    for lev in range(N_LEVELS):
        h = 1 << lev
        start = idx // (2 * h) * (2 * h)
        upper = (idx - start) >= h
        same = start[:, None] == start[None, :]
        masks.append((same & upper[:, None] & ~upper[None, :]).astype(np.float64))
    m_f = np.stack(masks, axis=0)
    m_b = m_f[:, ::-1, ::-1]
    t_f = np.concatenate([tri, tri], axis=1)
    t_b = np.concatenate([tri[::-1, ::-1]] * 2, axis=1)
    return (jnp.asarray(t_f, dtype=BF16), jnp.asarray(t_b, dtype=BF16),
            jnp.asarray(m_f, dtype=F32), jnp.asarray(m_b, dtype=F32))


def _level_exponents(b, g, fwd):
    c, kk = b.shape
    tile = (c // SUBLANES, SUBLANES, kk)
    b8 = b.reshape(tile)
    r8 = lax.broadcasted_iota(jnp.int32, tile, 1)
    out = [jnp.where(r8 % 2 == (1 if fwd else 0), g.reshape(tile), 0.0).reshape(c, kk)]
    lo, hi = (1, 5) if fwd else (2, 6)
    mids = [jnp.where(r8 < 4, b8[:, lo:lo + 1, :], b8[:, hi:hi + 1, :]), b8[:, 3:4, :] if fwd else b8[:, 4:5, :]]
    for lev, mid in zip((1, 2), mids):
        h = 1 << lev
        second_half = (r8 % (2 * h)) >= h
        sign = jnp.where(second_half if fwd else jnp.logical_not(second_half), 1.0, -1.0)
        out.append(((b8 - mid) * sign).reshape(c, kk))
    for lev in range(3, N_LEVELS):
        h = 1 << lev
        bb = b.reshape(c // (2 * h), 2 * h, kk)
        first, second = bb[:, :h, :], bb[:, h:, :]
        if fwd:
            mid = bb[:, h - 1:h, :]
            t = jnp.concatenate([mid - first, second - mid], axis=1)
        else:
            mid = bb[:, h:h + 1, :]
            t = jnp.concatenate([first - mid, mid - second], axis=1)
        out.append(t.reshape(c, kk))
    return out


def _scan_chunks(chains, from_zero=False):
    c = CHUNK

    st1 = []
    for q_ref, k_ref, g_ref, v_ref, tri_ref, m_ref, s_ref, o_ref, rows, kcols, vcols, fwd in chains:
        g = g_ref[rows, kcols]
        g_hi = g.astype(BF16)
        g_lo = (g - g_hi.astype(F32)).astype(BF16)
        b = _dot(tri_ref[...], jnp.concatenate([g_hi, g_lo], axis=0))
        diag = _dot_nt(q_ref[rows, kcols], k_ref[rows, kcols])
        st1.append((g, b, diag))

    st2 = []
    for (q_ref, k_ref, g_ref, v_ref, tri_ref, m_ref, s_ref, o_ref, rows, kcols, vcols, fwd), (g, b, diag) in zip(
            chains, st1):
        q_bf = q_ref[rows, kcols]
        k_bf = k_ref[rows, kcols]
        scores = m_ref[0] * diag
        for lev, t in enumerate(_level_exponents(b, g, fwd)):
            e = jnp.exp2(t).astype(BF16)
            scores += m_ref[lev + 1] * _dot_nt(q_bf * e, k_bf * e)
        st2.append(scores)

    for (q_ref, k_ref, g_ref, v_ref, tri_ref, m_ref, s_ref, o_ref, rows, kcols, vcols, fwd), (g, b, diag), scores in zip(
            chains, st1, st2):
        o = _dot(scores.astype(BF16), v_ref[rows, vcols])
        if not from_zero:
            qe = q_ref[rows, kcols] * jnp.exp2(b).astype(BF16)
            o = _dot(qe, s_ref[...].astype(BF16)) + o
        o_ref[rows, vcols] = o

    for (q_ref, k_ref, g_ref, v_ref, tri_ref, m_ref, s_ref, o_ref, rows, kcols, vcols, fwd), (g, b, diag) in zip(
            chains, st1):
        last = c - 1 if fwd else 0
        b_last = b[last:last + 1]
        kt = k_ref[rows, kcols] * jnp.exp2(b_last - b).astype(BF16)
        kv = _dot_tn(kt, v_ref[rows, vcols])
        if from_zero:
            s_ref[...] = kv
        else:
            e_col = jnp.broadcast_to(jnp.exp2(b_last), (LANES, b.shape[1])).T
            s = s_ref[...]
            decay = jnp.concatenate([e_col] * (s.shape[1] // LANES), axis=1)
            s_ref[...] = decay * s + kv


def _gla_scan_kernel(*refs, has_s0, has_sout, head_k, head_v):
    qf_ref, kf_ref, qb_ref, kb_ref, v_ref, r_ref, gf_ref, gb_ref = refs[:8]
    pos = 8
    if has_s0:
        s0_ref = refs[pos]
        pos += 1
    tf_ref, tb_ref, mf_ref, mb_ref, gn_ref = refs[pos:pos + 5]
    pos += 5
    o_ref = refs[pos]
    pos += 1
    if has_sout:
        s_ref, of_ref, ob_ref = refs[pos:pos + 3]
    else:
        of_ref, ob_ref, s_ref = refs[pos:pos + 3]

    t = v_ref.shape[0]
    n = t // CHUNK
    hps = v_ref.shape[1] // head_v
    def advance(ci, from_zero=False):
        aligned = (lambda r: r) if isinstance(ci, int) else (lambda r: pl.multiple_of(r, CHUNK))
        rows_f = pl.ds(aligned(ci * CHUNK), CHUNK)
        rows_b = pl.ds(aligned((n - 1 - ci) * CHUNK), CHUNK)
        chains = []
        for hh in range(hps):
            kcols = slice(hh * head_k, (hh + 1) * head_k)
            vcols = slice(hh * head_v, (hh + 1) * head_v)
            chains.append((qf_ref, kf_ref, gf_ref, v_ref, tf_ref, mf_ref, s_ref.at[0, hh], of_ref, rows_f,
                           kcols, vcols, True))
            chains.append((qb_ref, kb_ref, gb_ref, v_ref, tb_ref, mb_ref, s_ref.at[1, hh], ob_ref, rows_b,
                           kcols, vcols, False))
        _scan_chunks(chains, from_zero)

    def body(ci, carry):
        advance(ci)
        return carry

    if has_s0:
        s_ref[...] = s0_ref[...]
        first = 0
    else:
        advance(0, from_zero=True)
        first = 1
    lax.fori_loop(first, n, body, 0, unroll=min(n, 4))

    def norm_body(ci, carry):
        rows = pl.ds(pl.multiple_of(ci * CHUNK, CHUNK), CHUNK)
        for hh in range(hps):
            vcols = slice(hh * head_v, (hh + 1) * head_v)
            o = of_ref[rows, vcols] + ob_ref[rows, vcols]
            r = r_ref[rows, vcols].astype(F32)
            o_ref[rows, vcols] = (_rms(o) * gn_ref[...] * (r * jax.nn.sigmoid(r))).astype(BF16)
        return carry

    lax.fori_loop(0, n, norm_body, 0, unroll=min(n, 4))


def _gla_scan(p, g, gnorm, s0, *, t, nb, row_block0, heads, hps, head_k, head_v, want_state):
    assert heads % hps == 0
    nhb = heads // hps
    t_f, t_b, m_f, m_b = _scan_constants()
    qk_blk = lambda off: pl.BlockSpec((t, hps * head_k), lambda b, h: (row_block0 + b, off * nhb + h))
    v_off = 4 * heads * head_k // (hps * head_v)
    state_blk = pl.BlockSpec((None, None, 2, hps, head_k, head_v), lambda b, h: (b, 0, 0, h, 0, 0))
    in_specs = [
        qk_blk(0), qk_blk(1), qk_blk(2), qk_blk(3),
        pl.BlockSpec((t, hps * head_v), lambda b, h: (row_block0 + b, v_off + h)),
        pl.BlockSpec((t, hps * head_v), lambda b, h: (row_block0 + b, v_off + nhb + h)),
        pl.BlockSpec((t, hps * head_k), lambda b, h: (row_block0 + b, h)),
        pl.BlockSpec((t, hps * head_k), lambda b, h: (row_block0 + b, nhb + h)),
    ]
    args = [p, p, p, p, p, p, g, g]
    if s0 is not None:
        in_specs.append(state_blk)
        args.append(s0)
    in_specs += [
        pl.BlockSpec(t_f.shape, lambda b, h: (0, 0)),
        pl.BlockSpec(t_b.shape, lambda b, h: (0, 0)),
        pl.BlockSpec(m_f.shape, lambda b, h: (0, 0, 0)),
        pl.BlockSpec(m_b.shape, lambda b, h: (0, 0, 0)),
        pl.BlockSpec((1, head_v), lambda b, h: (0, 0)),
    ]
    args += [t_f, t_b, m_f, m_b, gnorm]
    out_specs = [pl.BlockSpec((t, hps * head_v), lambda b, h: (b, h))]
    out_shape = [jax.ShapeDtypeStruct((nb * t, heads * head_v), BF16)]
    if want_state:
        out_specs.append(state_blk)
        out_shape.append(jax.ShapeDtypeStruct((nb, 1, 2, heads, head_k, head_v), F32))
    return pl.pallas_call(
        functools.partial(_gla_scan_kernel, has_s0=s0 is not None, has_sout=want_state, head_k=head_k,
                          head_v=head_v),
        grid=(nb, nhb),
        in_specs=in_specs,
        out_specs=out_specs,
        out_shape=out_shape,
        scratch_shapes=[
            pltpu.VMEM((t, hps * head_v), F32),
            pltpu.VMEM((t, hps * head_v), F32),
        ] + ([] if want_state else [pltpu.VMEM((2, hps, head_k, head_v), F32)]),
        compiler_params=_params(
            "arbitrary", "arbitrary",
            vmem_bytes=2 * t * hps * (4 * head_k * 2 + 3 * head_v * 2 + 2 * head_k * 4)
            + (2 * (int(s0 is not None) + int(want_state)) + int(not want_state)) * 2 * hps * head_k * head_v * 4
            + 2 * t * hps * head_v * 4 + 8 * MIB),
        name="gla_scan",
    )(*args)


def kernel(x_prompt, x_sample, state_gla, c, c_ctx, ada_w, ada_b, norm_pre, norm_post, ffn_w_gate_up,
           ffn_w_down, fourier_w, gla_w_in, gla_w_gate_up, gla_b_gate, gla_norm, gla_w_out):
    batch, seq, d = x_prompt.shape
    dec_batch, dec_seq, _ = x_sample.shape
    depth = ada_w.shape[0]
    heads, head_k, head_v = state_gla.shape[3:]
    dk_tot = heads * head_k
    dv_tot = heads * head_v
    rank = gla_w_gate_up.shape[2]
    d_ff = ffn_w_down.shape[2]
    ctx_rows = batch * seq
    assert ctx_rows % dec_seq == 0 and dec_seq % seq == 0 and 1 + dec_batch <= SUBLANES
    n_ctx_groups = ctx_rows // dec_seq
    grp = dict(group_rows=dec_seq, n_ctx_groups=n_ctx_groups)
    tm_small = min(dec_seq, 512)
    tm_big = min(dec_seq, 1024)
    tf = 512 if d_ff % 512 == 0 else d_ff

    cond8 = jnp.zeros((SUBLANES, d), F32).at[0].set(c_ctx).at[1:1 + dec_batch].set(c)
    mods = _modulation(cond8, ada_w, ada_b)[:, :1 + dec_batch].reshape(depth, 1 + dec_batch, N_MOD, d)

    n_main = 4 * dk_tot + 2 * dv_tot
    w_in_t = jnp.swapaxes(gla_w_in, 1, 2)

    xs = (x_prompt.reshape(ctx_rows, d), x_sample.reshape(dec_batch * dec_seq, d))
    states = []
    for l in range(depth):
        mod = mods[l]
        j = l // 2
        npre = lambda s: norm_pre[l, s][None, :]
        npost = lambda s: norm_post[l, s][None, :]
        ffn = functools.partial(_ffn, mod=mod, wgu=ffn_w_gate_up, wd=ffn_w_down, layer=l, tf=tf, tm=tm_big,
                                **grp)
        (x,) = ffn(xs, gpre=npre(0), gpost=npost(0), slot=0, k=0, split_out=False)
        if l % 2 == 0:
            f_ctx = _fourier(x, mod, npre(1), t=seq, nb=batch, row_block0=0, mod0=lambda b: 0, k=1)
            f_lat = _fourier(x, mod, npre(1), t=dec_seq, nb=dec_batch, row_block0=n_ctx_groups,
                             mod0=lambda b: 1 + b, k=1)
            x = _dense_post((f_ctx, f_lat), fourier_w, x, mod, npost(1), widx=j, k=1, tm=tm_small, **grp)
        else:
            zero = jnp.zeros((rank, dk_tot), F32)
            wz = jnp.concatenate([jnp.concatenate([gla_w_gate_up[j, 0], zero], axis=1),
                                  jnp.concatenate([zero, gla_w_gate_up[j, 1]], axis=1)], axis=0).astype(BF16)
            bz = gla_b_gate[j].reshape(1, 2 * dk_tot)
            colscale = jnp.ones((n_main,), F32).at[:dk_tot].set(head_k ** -0.5)
            colscale = colscale.at[2 * dk_tot:3 * dk_tot].set(head_k ** -0.5).reshape(1, n_main)
            p, g = _gla_in(x, mod, npre(1), w_in_t, wz, bz, colscale, widx=j, n_main=n_main, k=1,
                           tm=tm_big, n_steps=8, head_k=head_k, dk_tot=dk_tot, **grp)
            gn = gla_norm[j][None, :]
            scan = functools.partial(_gla_scan, p, g, gn, heads=heads, head_k=head_k, head_v=head_v)
            o_ctx, st = scan(None, t=seq, nb=batch, row_block0=0, hps=heads, want_state=True)
            (o_lat,) = scan(state_gla[:, j:j + 1], t=dec_seq, nb=dec_batch, row_block0=n_ctx_groups,
                            hps=min(heads, max(1, 2 * heads * seq // dec_seq)), want_state=False)
            states.append(st)
            x = _dense_post((o_ctx, o_lat), gla_w_out, x, mod, npost(1), widx=j, k=1, tm=tm_small, **grp)
        xs = tuple(ffn((x,), gpre=npre(2), gpost=npost(2), slot=1, k=2, split_out=l == depth - 1))

    y_prompt = xs[0].reshape(batch, seq, d)
    y_sample = xs[1].reshape(dec_batch, dec_seq, d)
    new_state = jnp.concatenate(states, axis=1)
    return (y_prompt, y_sample, new_state)
```

```python
import functools

import numpy as np
import jax
import jax.numpy as jnp
from jax import lax
from jax.experimental import pallas as pl
from jax.experimental.pallas import tpu as pltpu

F32 = jnp.float32
BF16 = jnp.bfloat16

EPS = 1e-6
N_MOD = 9
GRID_W = 64
FOURIER_GROUPS = 4
GATE_TAU = 16.0
CHUNK = 64
ROPE_BASE = 10000.0
LOG2_E = 1.4426950408889634
N_LEVELS = 6
assert CHUNK == 1 << N_LEVELS

V7X_VMEM_BYTES = 64 * 1024 * 1024
MIB = 1024 * 1024
LANES = 128
SUBLANES = 8


def _params(*sem, vmem_bytes):
    request = (int(vmem_bytes) // MIB + 2) * MIB
    assert request < V7X_VMEM_BYTES, request
    return pltpu.CompilerParams(dimension_semantics=sem, vmem_limit_bytes=request)


def _dot(a, b):
    return jnp.dot(a, b, preferred_element_type=F32)


def _dot_nt(a, b):
    return lax.dot_general(a, b, (((1,), (1,)), ((), ())), preferred_element_type=F32)


def _dot_tn(a, b):
    return lax.dot_general(a, b, (((0,), (0,)), ((), ())), preferred_element_type=F32)


def _rms(x):
    return x * lax.rsqrt(jnp.mean(x * x, axis=-1, keepdims=True) + EPS)


def _pre_mod(x, gpre, mod_ref, k):
    gs = gpre * (1.0 + mod_ref[3 * k + 1:3 * k + 2, :])
    return _rms(x) * gs + mod_ref[3 * k:3 * k + 1, :]


def _post_add(x, out, gpost, mod_ref, k, res_w):
    gg = (res_w * mod_ref[3 * k + 2:3 * k + 3, :]) * gpost
    return x + _rms(out) * gg


ROW_BLOCK = 128


def _for_row_blocks(n_rows, fn, row_block=ROW_BLOCK):
    block = min(row_block, n_rows)

    def body(r, carry):
        fn(pl.ds(pl.multiple_of(r * block, block), block))
        return carry

    lax.fori_loop(0, n_rows // block, body, 0)


def _mod_index(i, tm, group_rows, n_ctx_groups):
    return jnp.maximum(i * tm // group_rows - (n_ctx_groups - 1), 0)


def _on_tile_ref(refs, i, n_ctx_tiles, fn):
    if len(refs) == 1:
        fn(refs[0])
    else:
        pl.when(i < n_ctx_tiles)(lambda: fn(refs[0]))
        pl.when(i >= n_ctx_tiles)(lambda: fn(refs[1]))


def _split_row_specs(arrays, tm, width, n_ctx_tiles):
    if len(arrays) == 1:
        return [pl.BlockSpec((tm, width), lambda i: (i, 0))]
    return [
        pl.BlockSpec((tm, width), lambda i: (jnp.minimum(i, n_ctx_tiles - 1), 0)),
        pl.BlockSpec((tm, width), lambda i: (jnp.maximum(i - n_ctx_tiles, 0), 0)),
    ]


def _mod_kernel(c_ref, w_ref, b_ref, o_ref):
    c = c_ref[...]
    s = (c * jax.nn.sigmoid(c)).astype(BF16)
    o_ref[...] = _dot(s, w_ref[...].astype(BF16)) + b_ref[...]


def _modulation(cond8, ada_w, ada_b):
    depth, d, n = ada_w.shape
    tn = min(n, 1024)
    return pl.pallas_call(
        _mod_kernel,
        grid=(depth, n // tn),
        in_specs=[
            pl.BlockSpec((SUBLANES, d), lambda l, j: (0, 0)),
            pl.BlockSpec((None, d, tn), lambda l, j: (l, 0, j)),
            pl.BlockSpec((None, 1, tn), lambda l, j: (l, 0, j)),
        ],
        out_specs=pl.BlockSpec((None, SUBLANES, tn), lambda l, j: (l, 0, j)),
        out_shape=jax.ShapeDtypeStruct((depth, SUBLANES, n), F32),
        compiler_params=_params("arbitrary", "arbitrary", vmem_bytes=d * tn * (2 * 4 + 2) + 2 * MIB),
        name="modulation",
    )(cond8, ada_w, ada_b.reshape(depth, 1, n))


def _on_tile_part(parts, tile, n_ctx_tiles, fn):
    if len(parts) == 1:
        fn(parts[0], tile)
    else:
        pl.when(tile < n_ctx_tiles)(lambda: fn(parts[0], tile))
        pl.when(tile >= n_ctx_tiles)(lambda: fn(parts[1], tile - n_ctx_tiles))


def _ffn_kernel(*refs, n_x, n_out, n_ctx_tiles, k, res_w):
    x_hbm = refs[:n_x]
    mod_ref, gpre_ref, gpost_ref, wg_ref, wu_ref, wd_ref = refs[n_x:n_x + 6]
    o_hbm = refs[n_x + 6:n_x + 6 + n_out]
    h_ref, x_buf, acc_ref, out_sem, x_sem = refs[n_x + 6 + n_out:]
    i = pl.program_id(0)
    j = pl.program_id(1)
    tm = x_buf.shape[0]
    n_blk = x_sem.shape[0]
    blk = tm // n_blk

    def out_copy(ref, tile):
        rows = pl.ds(pl.multiple_of(tile * tm, tm), tm)
        return pltpu.make_async_copy(acc_ref, ref.at[rows, :], out_sem)

    def x_copy(ref, tile, b):
        src = pl.ds(pl.multiple_of(tile * tm + b * blk, blk), blk)
        dst = pl.ds(pl.multiple_of(b * blk, blk), blk)
        return pltpu.make_async_copy(ref.at[src, :], x_buf.at[dst, :], x_sem.at[b])

    @pl.when(j == 0)
    def _():
        def fetch_and_prologue(ref, t):
            for b in range(n_blk):
                x_copy(ref, t, b).start()

            def body(b, carry):
                x_copy(ref, t, b).wait()
                rows = pl.ds(pl.multiple_of(b * blk, blk), blk)
                h_ref[rows, :] = _pre_mod(x_buf[rows, :], gpre_ref[...], mod_ref, k).astype(BF16)
                return carry

            lax.fori_loop(0, n_blk, body, 0)

        _on_tile_part(x_hbm, i, n_ctx_tiles, fetch_and_prologue)

        @pl.when(i > 0)
        def _():
            _on_tile_part(o_hbm, i - 1, n_ctx_tiles, lambda ref, t: out_copy(ref, t).wait())

    def hidden_tile(first):
        h = h_ref[...]
        half = wg_ref.shape[1] // 2
        acts = []
        for c0 in (0, half):
            g = _dot(h, wg_ref[:, c0:c0 + half].astype(BF16))
            u = _dot(h, wu_ref[:, c0:c0 + half].astype(BF16))
            acts.append((g * jax.nn.sigmoid(g) * u).astype(BF16))
        y = _dot(jnp.concatenate(acts, axis=1), wd_ref[...].astype(BF16))
        if first:
            acc_ref[...] = y
        else:
            acc_ref[...] += y

    pl.when(j == 0)(lambda: hidden_tile(True))
    pl.when(j > 0)(lambda: hidden_tile(False))

    @pl.when(j == pl.num_programs(1) - 1)
    def _():
        def epilogue(rows):
            acc_ref[rows, :] = _post_add(x_buf[rows, :], acc_ref[rows, :], gpost_ref[...], mod_ref, k, res_w)

        _for_row_blocks(tm, epilogue)
        _on_tile_part(o_hbm, i, n_ctx_tiles, lambda ref, t: out_copy(ref, t).start())

        @pl.when(i == pl.num_programs(0) - 1)
        def _():
            _on_tile_part(o_hbm, i, n_ctx_tiles, lambda ref, t: out_copy(ref, t).wait())


def _ffn(xs, mod, gpre, gpost, wgu, wd, *, layer, slot, k, tm, tf, split_out, group_rows, n_ctx_groups):
    m = sum(x.shape[0] for x in xs)
    d = xs[0].shape[1]
    nf = wd.shape[2] // tf
    n_ctx_tiles = n_ctx_groups * group_rows // tm
    out_rows = (n_ctx_tiles * tm, m - n_ctx_tiles * tm) if split_out else (m,)
    midx = functools.partial(_mod_index, tm=tm, group_rows=group_rows, n_ctx_groups=n_ctx_groups)
    hbm = pl.BlockSpec(memory_space=pl.ANY)
    return pl.pallas_call(
        functools.partial(_ffn_kernel, n_x=len(xs), n_out=len(out_rows), n_ctx_tiles=n_ctx_tiles, k=k,
                          res_w=0.5),
        grid=(m // tm, nf),
        in_specs=[hbm] * len(xs) + [
            pl.BlockSpec((None, N_MOD, d), lambda i, j: (midx(i), 0, 0)),
            pl.BlockSpec((1, d), lambda i, j: (0, 0)),
            pl.BlockSpec((1, d), lambda i, j: (0, 0)),
            pl.BlockSpec((None, None, d, tf), lambda i, j: (layer, slot, 0, j)),
            pl.BlockSpec((None, None, d, tf), lambda i, j: (layer, slot, 0, j + nf)),
            pl.BlockSpec((None, None, tf, d), lambda i, j: (layer, slot, j, 0)),
        ],
        out_specs=[hbm] * len(out_rows),
        out_shape=[jax.ShapeDtypeStruct((r, d), F32) for r in out_rows],
        scratch_shapes=[
            pltpu.VMEM((tm, d), BF16),
            pltpu.VMEM((tm, d), F32),
            pltpu.VMEM((tm, d), F32),
            pltpu.SemaphoreType.DMA(()),
            pltpu.SemaphoreType.DMA((tm // min(ROW_BLOCK, tm),)),
        ],
        compiler_params=_params("arbitrary", "arbitrary",
                                vmem_bytes=tm * d * (2 + 4 + 4) + 2 * 3 * d * tf * 4
                                + (tf // 2) * (tm * (4 + 4 + 2) + 3 * d * 2) + 2 * MIB),
        name="ffn",
    )(*xs, mod, gpre, gpost, wgu, wgu, wd)


def _dense_post_kernel(*refs, n_a, n_ctx_tiles, k, res_w):
    a_refs = refs[:n_a]
    w_ref, x_ref, mod_ref, gpost_ref, o_ref, wbf_ref = refs[n_a:]

    @pl.when(pl.program_id(0) == 0)
    def _():
        wbf_ref[...] = w_ref[...].astype(BF16)

    def body(a_ref):
        y = _dot(a_ref[...], wbf_ref[...])
        o_ref[...] = _post_add(x_ref[...], y, gpost_ref[...], mod_ref, k, res_w)

    _on_tile_ref(a_refs, pl.program_id(0), n_ctx_tiles, body)


def _dense_post(a_parts, w, x, mod, gpost, *, widx, k, tm, group_rows, n_ctx_groups, tile0=0):
    d = x.shape[1]
    m = sum(a.shape[0] for a in a_parts)
    kk = a_parts[0].shape[1]
    n_ctx_tiles = n_ctx_groups * group_rows // tm
    midx = functools.partial(_mod_index, tm=tm, group_rows=group_rows, n_ctx_groups=n_ctx_groups)
    return pl.pallas_call(
        functools.partial(_dense_post_kernel, n_a=len(a_parts), n_ctx_tiles=n_ctx_tiles, k=k, res_w=1.0),
        grid=(m // tm,),
        in_specs=_split_row_specs(a_parts, tm, kk, n_ctx_tiles) + [
            pl.BlockSpec((None, kk, d), lambda i: (widx, 0, 0), pipeline_mode=pl.Buffered(1)),
            pl.BlockSpec((tm, d), lambda i: (tile0 + i, 0)),
            pl.BlockSpec((None, N_MOD, d), lambda i: (midx(tile0 + i), 0, 0)),
            pl.BlockSpec((1, d), lambda i: (0, 0)),
        ],
        out_specs=pl.BlockSpec((tm, d), lambda i: (i, 0)),
        out_shape=jax.ShapeDtypeStruct((m, d), F32),
        scratch_shapes=[pltpu.VMEM((kk, d), BF16)],
        compiler_params=_params("arbitrary", vmem_bytes=2 * (len(a_parts) * tm * kk * 2 + 2 * tm * d * 4)
                                + kk * d * (4 + 2) + 2 * tm * d * 4 + 2 * MIB),
        name="dense_post",
    )(*a_parts, w, x, mod, gpost)


def _dft_tables(t, w):
    def cs(n):
        kn = np.outer(np.arange(n), np.arange(n)) % n
        ang = 2.0 * np.pi * kn.astype(np.float64) / n
        return np.cos(ang) / np.sqrt(n), np.sin(ang) / np.sqrt(n)

    ct, st = cs(t)
    cw, sw = cs(w)
    as_bf16 = lambda a: jnp.asarray(a, dtype=F32).astype(BF16)
    return as_bf16(np.concatenate([ct, st], axis=0)), as_bf16(cw), as_bf16(-sw)


def _fourier_kernel(x_ref, mod_ref, gpre_ref, cst_ref, cw_ref, swn_ref, *rest, k, mixed):
    if mixed:
        w_ref, gpost_ref, o_ref, f_ref, wbf_ref = rest

        @pl.when(pl.program_id(0) == 0)
        def _():
            wbf_ref[...] = w_ref[...].astype(BF16)
    else:
        (o_ref,) = rest
        f_ref = o_ref
    t = cst_ref.shape[1]
    w = cw_ref.shape[0]
    cst = cst_ref[...]
    for s in range(x_ref.shape[0] // t):
        rows = slice(s * t, (s + 1) * t)
        h = _pre_mod(x_ref[rows, :], gpre_ref[...], mod_ref, k).astype(BF16)
        for g in range(FOURIER_GROUPS):
            p = _dot(cst, h[:, g * w:(g + 1) * w]).astype(BF16)
            f = _dot(p[:t], cw_ref[...]) + _dot(p[t:], swn_ref[...])
            f_ref[rows, g * w:(g + 1) * w] = f.astype(BF16)
    if mixed:
        y = _dot(f_ref[...], wbf_ref[...])
        o_ref[...] = _post_add(x_ref[...], y, gpost_ref[...], mod_ref, k, 1.0)


def _fourier(x, mod, gpre, *, t, nb, row_block0, mod0, k, mix=None, seqs=1):
    d = x.shape[1]
    w = d // FOURIER_GROUPS
    cst, cw, swn = _dft_tables(t, w)
    assert nb % seqs == 0 and row_block0 % seqs == 0
    seq_t, t, nb, row_block0 = t, seqs * t, nb // seqs, row_block0 // seqs
    in_specs = [
        pl.BlockSpec((t, d), lambda b: (row_block0 + b, 0)),
        pl.BlockSpec((None, N_MOD, d), lambda b: (mod0(b), 0, 0)),
        pl.BlockSpec((1, d), lambda b: (0, 0)),
        pl.BlockSpec((2 * seq_t, seq_t), lambda b: (0, 0)),
        pl.BlockSpec((w, w), lambda b: (0, 0)),
        pl.BlockSpec((w, w), lambda b: (0, 0)),
    ]
    args = [x, mod, gpre, cst, cw, swn]
    vmem = 2 * (t * d * 4 + 2 * seq_t * seq_t * 2 + 2 * w * w * 2 + t * d * 2) + t * d * 2 + 2 * seq_t * w * 8
    scratch = []
    out_dtype = BF16
    if mix is not None:
        w_mix, widx, gpost = mix
        in_specs += [pl.BlockSpec((None, d, d), lambda b: (widx, 0, 0), pipeline_mode=pl.Buffered(1)),
                     pl.BlockSpec((1, d), lambda b: (0, 0))]
        args += [w_mix, gpost]
        scratch = [pltpu.VMEM((t, d), BF16), pltpu.VMEM((d, d), BF16)]
        out_dtype = F32
        vmem += d * d * (4 + 2) + 3 * t * d * 4 + 2 * MIB
    return pl.pallas_call(
        functools.partial(_fourier_kernel, k=k, mixed=mix is not None),
        grid=(nb,),
        in_specs=in_specs,
        out_specs=pl.BlockSpec((t, d), lambda b: (b, 0)),
        out_shape=jax.ShapeDtypeStruct((nb * t, d), out_dtype),
        scratch_shapes=scratch,
        compiler_params=_params("arbitrary", vmem_bytes=vmem),
        name="fourier",
    )(*args)


def _rope_tables(t, head_k):
    rows = t // GRID_W
    pairs = head_k // 4
    row = np.repeat(np.arange(rows), GRID_W).astype(np.float64)
    col = np.tile(np.arange(GRID_W), rows).astype(np.float64)
    inv = ROPE_BASE ** (-np.arange(pairs, dtype=np.float64) / pairs)
    ang = np.concatenate([row[:, None] * inv, col[:, None] * inv], axis=-1)
    cos, sin = np.cos(ang), np.sin(ang)
    ce = np.repeat(cos, 2, axis=-1)
    se = np.stack([-sin, sin], axis=-1).reshape(t, head_k)
    return jnp.asarray(ce, dtype=F32), jnp.asarray(se, dtype=F32)


def _gla_in_kernel(x_ref, mod_ref, gpre_ref, w_ref, wlr_ref, wz_ref, bz_ref, cs_ref, ce_ref, se_ref,
                   p_ref, g_ref, h_ref, lr_ref, *, k, n_ctx_tiles, n_qk_steps, head_k):
    i = pl.program_id(0)
    j = pl.program_id(1)

    @pl.when(j == 0)
    def _():
        def prologue(rows):
            h = _pre_mod(x_ref[rows, :], gpre_ref[...], mod_ref, k).astype(BF16)
            h_ref[rows, :] = h
            lr_ref[rows, :] = _dot_nt(h, wlr_ref[...].astype(BF16)).astype(BF16)

        _for_row_blocks(x_ref.shape[0], prologue, 2 * ROW_BLOCK)

    def step(rope):
        z = _dot(lr_ref[...], wz_ref[...]) + bz_ref[...]
        g_ref[...] = (jnp.minimum(z, 0.0) - jnp.log1p(jnp.exp(-jnp.abs(z)))) * (LOG2_E / GATE_TAU)
        h = h_ref[...]
        for s in range(w_ref.shape[0] // head_k):
            cols = slice(s * head_k, (s + 1) * head_k)
            acc = _dot_nt(h, w_ref[cols, :].astype(BF16)) * cs_ref[:, cols]
            if rope:
                lane = lax.broadcasted_iota(jnp.int32, acc.shape, 1)
                partner = jnp.where(lane % 2 == 0, pltpu.roll(acc, head_k - 1, 1), pltpu.roll(acc, 1, 1))
                acc = acc * ce_ref[...] + partner * se_ref[...]
            p_ref[:, cols] = acc.astype(BF16)

    rope = jnp.logical_and(i >= n_ctx_tiles, j < n_qk_steps)
    pl.when(rope)(lambda: step(True))
    pl.when(jnp.logical_not(rope))(lambda: step(False))


def _gla_in(x, mod, gpre, w_in_t, wz, bz, colscale, *, widx, n_main, k, tm, n_steps, head_k, dk_tot,
            group_rows, n_ctx_groups):
    m, d = x.shape
    n = n_main
    tn = n // n_steps
    gcols = 2 * dk_tot // n_steps
    n_lr = w_in_t.shape[1] - n_main
    assert (4 * dk_tot) % tn == 0 and tn % head_k == 0 and gcols % LANES == 0 and n_main % n_lr == 0
    n_ctx_tiles = n_ctx_groups * group_rows // tm
    tiles_per_group = group_rows // tm
    ce, se = _rope_tables(group_rows, head_k)
    midx = functools.partial(_mod_index, tm=tm, group_rows=group_rows, n_ctx_groups=n_ctx_groups)

    def tab_idx(i, j):
        return (jnp.where(i >= n_ctx_tiles, (i - n_ctx_tiles) % tiles_per_group, 0), 0)

    return pl.pallas_call(
        functools.partial(_gla_in_kernel, k=k, n_ctx_tiles=n_ctx_tiles, n_qk_steps=4 * dk_tot // tn,
                          head_k=head_k),
        grid=(m // tm, n_steps),
        in_specs=[
            pl.BlockSpec((tm, d), lambda i, j: (i, 0)),
            pl.BlockSpec((None, N_MOD, d), lambda i, j: (midx(i), 0, 0)),
            pl.BlockSpec((1, d), lambda i, j: (0, 0)),
            pl.BlockSpec((None, tn, d), lambda i, j: (widx, j, 0)),
            pl.BlockSpec((None, n_lr, d), lambda i, j: (widx, n_main // n_lr, 0)),
            pl.BlockSpec((wz.shape[0], gcols), lambda i, j: (0, j)),
            pl.BlockSpec((1, gcols), lambda i, j: (0, j)),
            pl.BlockSpec((1, tn), lambda i, j: (0, j)),
            pl.BlockSpec((tm, head_k), tab_idx),
            pl.BlockSpec((tm, head_k), tab_idx),
        ],
        out_specs=[
            pl.BlockSpec((tm, tn), lambda i, j: (i, j)),
            pl.BlockSpec((tm, gcols), lambda i, j: (i, j)),
        ],
        out_shape=[
            jax.ShapeDtypeStruct((m, n), BF16),
            jax.ShapeDtypeStruct((m, 2 * dk_tot), F32),
        ],
        scratch_shapes=[pltpu.VMEM((tm, d), BF16), pltpu.VMEM((tm, n_lr), BF16)],
        compiler_params=_params("arbitrary", "arbitrary",
                                vmem_bytes=2 * (tm * d * 4 + d * tn * 4 + tm * tn * 2 + tm * gcols * 4
                                                + 2 * tm * head_k * 4) + tm * d * 2
                                + 2 * head_k * (tm * 4 + d * 2) + 2 * tm * gcols * 4 + 2 * MIB),
        name="gla_in",
    )(x, mod, gpre, w_in_t, w_in_t, wz, bz, colscale, ce, se)


def _scan_constants():
    c = CHUNK
    idx = np.arange(c)
    tri = (idx[None, :] <= idx[:, None]).astype(np.float64)
    masks = [np.eye(c)]
    for lev in range(N_LEVELS):
        h = 1 << lev
        start = idx // (2 * h) * (2 * h)
        upper = (idx - start) >= h
        same = start[:, None] == start[None, :]
        masks.append((same & upper[:, None] & ~upper[None, :]).astype(np.float64))
    m_f = np.stack(masks, axis=0)
    m_b = m_f[:, ::-1, ::-1]
    t_f = np.concatenate([tri, tri], axis=1)
    t_b = np.concatenate([tri[::-1, ::-1]] * 2, axis=1)
    return (jnp.asarray(t_f, dtype=BF16), jnp.asarray(t_b, dtype=BF16),
            jnp.asarray(m_f, dtype=F32), jnp.asarray(m_b, dtype=F32))


def _level_exponents(b, g, fwd):
    c, kk = b.shape
    tile = (c // SUBLANES, SUBLANES, kk)
    b8 = b.reshape(tile)
    r8 = lax.broadcasted_iota(jnp.int32, tile, 1)
    out = [jnp.where(r8 % 2 == (1 if fwd else 0), g.reshape(tile), 0.0).reshape(c, kk)]
    lo, hi = (1, 5) if fwd else (2, 6)
    mids = [jnp.where(r8 < 4, b8[:, lo:lo + 1, :], b8[:, hi:hi + 1, :]), b8[:, 3:4, :] if fwd else b8[:, 4:5, :]]
    for lev, mid in zip((1, 2), mids):
        h = 1 << lev
        second_half = (r8 % (2 * h)) >= h
        sign = jnp.where(second_half if fwd else jnp.logical_not(second_half), 1.0, -1.0)
        out.append(((b8 - mid) * sign).reshape(c, kk))
    for lev in range(3, N_LEVELS):
        h = 1 << lev
        bb = b.reshape(c // (2 * h), 2 * h, kk)
        first, second = bb[:, :h, :], bb[:, h:, :]
        if fwd:
            mid = bb[:, h - 1:h, :]
            t = jnp.concatenate([mid - first, second - mid], axis=1)
        else:
            mid = bb[:, h:h + 1, :]
            t = jnp.concatenate([first - mid, mid - second], axis=1)
        out.append(t.reshape(c, kk))
    return out


def _scan_chunks(chains, from_zero=False):
    c = CHUNK

    st1 = []
    for q_ref, k_ref, g_ref, v_ref, tri_ref, m_ref, s_ref, o_ref, rows, kcols, vcols, fwd in chains:
        g = g_ref[rows, kcols]
        g_hi = g.astype(BF16)
        g_lo = (g - g_hi.astype(F32)).astype(BF16)
        b = _dot(tri_ref[...], jnp.concatenate([g_hi, g_lo], axis=0))
        diag = _dot_nt(q_ref[rows, kcols], k_ref[rows, kcols])
        st1.append((g, b, diag))

    st2 = []
    for (q_ref, k_ref, g_ref, v_ref, tri_ref, m_ref, s_ref, o_ref, rows, kcols, vcols, fwd), (g, b, diag) in zip(
            chains, st1):
        q_bf = q_ref[rows, kcols]
        k_bf = k_ref[rows, kcols]
        scores = m_ref[0] * diag
        for lev, t in enumerate(_level_exponents(b, g, fwd)):
            e = jnp.exp2(t).astype(BF16)
            scores += m_ref[lev + 1] * _dot_nt(q_bf * e, k_bf * e)
        st2.append(scores)

    for (q_ref, k_ref, g_ref, v_ref, tri_ref, m_ref, s_ref, o_ref, rows, kcols, vcols, fwd), (g, b, diag), scores in zip(
            chains, st1, st2):
        o = _dot(scores.astype(BF16), v_ref[rows, vcols])
        if not from_zero:
            qe = q_ref[rows, kcols] * jnp.exp2(b).astype(BF16)
            o = _dot(qe, s_ref[...].astype(BF16)) + o
        o_ref[rows, vcols] = o

    for (q_ref, k_ref, g_ref, v_ref, tri_ref, m_ref, s_ref, o_ref, rows, kcols, vcols, fwd), (g, b, diag) in zip(
            chains, st1):
        last = c - 1 if fwd else 0
        b_last = b[last:last + 1]
        kt = k_ref[rows, kcols] * jnp.exp2(b_last - b).astype(BF16)
        kv = _dot_tn(kt, v_ref[rows, vcols])
        if from_zero:
            s_ref[...] = kv
        else:
            e_col = jnp.broadcast_to(jnp.exp2(b_last), (LANES, b.shape[1])).T
            s = s_ref[...]
            decay = jnp.concatenate([e_col] * (s.shape[1] // LANES), axis=1)
            s_ref[...] = decay * s + kv


def _gla_scan_kernel(*refs, has_s0, has_sout, head_k, head_v):
    qf_ref, kf_ref, qb_ref, kb_ref, v_ref, r_ref, gf_ref, gb_ref = refs[:8]
    pos = 8
    if has_s0:
        s0_ref = refs[pos]
        pos += 1
    tf_ref, tb_ref, mf_ref, mb_ref, gn_ref = refs[pos:pos + 5]
    pos += 5
    o_ref = refs[pos]
    pos += 1
    if has_sout:
        s_ref, of_ref, ob_ref = refs[pos:pos + 3]
    else:
        of_ref, ob_ref, s_ref = refs[pos:pos + 3]

    t = v_ref.shape[0]
    n = t // CHUNK
    hps = v_ref.shape[1] // head_v
    def advance(ci, from_zero=False):
        aligned = (lambda r: r) if isinstance(ci, int) else (lambda r: pl.multiple_of(r, CHUNK))
        rows_f = pl.ds(aligned(ci * CHUNK), CHUNK)
        rows_b = pl.ds(aligned((n - 1 - ci) * CHUNK), CHUNK)
        chains = []
        for hh in range(hps):
            kcols = slice(hh * head_k, (hh + 1) * head_k)
            vcols = slice(hh * head_v, (hh + 1) * head_v)
            chains.append((qf_ref, kf_ref, gf_ref, v_ref, tf_ref, mf_ref, s_ref.at[0, hh], of_ref, rows_f,
                           kcols, vcols, True))
            chains.append((qb_ref, kb_ref, gb_ref, v_ref, tb_ref, mb_ref, s_ref.at[1, hh], ob_ref, rows_b,
                           kcols, vcols, False))
        _scan_chunks(chains, from_zero)

    def body(ci, carry):
        advance(ci)
        return carry

    if has_s0:
        s_ref[...] = s0_ref[...]
        first = 0
    else:
        advance(0, from_zero=True)
        first = 1
    lax.fori_loop(first, n, body, 0, unroll=min(n, 4))

    def norm_body(ci, carry):
        rows = pl.ds(pl.multiple_of(ci * CHUNK, CHUNK), CHUNK)
        for hh in range(hps):
            vcols = slice(hh * head_v, (hh + 1) * head_v)
            o = of_ref[rows, vcols] + ob_ref[rows, vcols]
            r = r_ref[rows, vcols].astype(F32)
            o_ref[rows, vcols] = (_rms(o) * gn_ref[...] * (r * jax.nn.sigmoid(r))).astype(BF16)
        return carry

    lax.fori_loop(0, n, norm_body, 0, unroll=min(n, 4))


def _gla_scan(p, g, gnorm, s0, *, t, nb, row_block0, heads, hps, head_k, head_v, want_state):
    assert heads % hps == 0
    nhb = heads // hps
    t_f, t_b, m_f, m_b = _scan_constants()
    qk_blk = lambda off: pl.BlockSpec((t, hps * head_k), lambda b, h: (row_block0 + b, off * nhb + h))
    v_off = 4 * heads * head_k // (hps * head_v)
    state_blk = pl.BlockSpec((None, None, 2, hps, head_k, head_v), lambda b, h: (b, 0, 0, h, 0, 0))
    in_specs = [
        qk_blk(0), qk_blk(1), qk_blk(2), qk_blk(3),
        pl.BlockSpec((t, hps * head_v), lambda b, h: (row_block0 + b, v_off + h)),
        pl.BlockSpec((t, hps * head_v), lambda b, h: (row_block0 + b, v_off + nhb + h)),
        pl.BlockSpec((t, hps * head_k), lambda b, h: (row_block0 + b, h)),
        pl.BlockSpec((t, hps * head_k), lambda b, h: (row_block0 + b, nhb + h)),
    ]
    args = [p, p, p, p, p, p, g, g]
    if s0 is not None:
        in_specs.append(state_blk)
        args.append(s0)
    in_specs += [
        pl.BlockSpec(t_f.shape, lambda b, h: (0, 0)),
        pl.BlockSpec(t_b.shape, lambda b, h: (0, 0)),
        pl.BlockSpec(m_f.shape, lambda b, h: (0, 0, 0)),
        pl.BlockSpec(m_b.shape, lambda b, h: (0, 0, 0)),
        pl.BlockSpec((1, head_v), lambda b, h: (0, 0)),
    ]
    args += [t_f, t_b, m_f, m_b, gnorm]
    out_specs = [pl.BlockSpec((t, hps * head_v), lambda b, h: (b, h))]
    out_shape = [jax.ShapeDtypeStruct((nb * t, heads * head_v), BF16)]
    if want_state:
        out_specs.append(state_blk)
        out_shape.append(jax.ShapeDtypeStruct((nb, 1, 2, heads, head_k, head_v), F32))
    return pl.pallas_call(
        functools.partial(_gla_scan_kernel, has_s0=s0 is not None, has_sout=want_state, head_k=head_k,
                          head_v=head_v),
        grid=(nb, nhb),
        in_specs=in_specs,
        out_specs=out_specs,
        out_shape=out_shape,
        scratch_shapes=[
            pltpu.VMEM((t, hps * head_v), F32),
            pltpu.VMEM((t, hps * head_v), F32),
        ] + ([] if want_state else [pltpu.VMEM((2, hps, head_k, head_v), F32)]),
        compiler_params=_params(
            "arbitrary", "arbitrary",
            vmem_bytes=2 * t * hps * (4 * head_k * 2 + 3 * head_v * 2 + 2 * head_k * 4)
            + (2 * (int(s0 is not None) + int(want_state)) + int(not want_state)) * 2 * hps * head_k * head_v * 4
            + 2 * t * hps * head_v * 4 + 8 * MIB),
        name="gla_scan",
    )(*args)


def kernel(x_prompt, x_sample, state_gla, c, c_ctx, ada_w, ada_b, norm_pre, norm_post, ffn_w_gate_up,
           ffn_w_down, fourier_w, gla_w_in, gla_w_gate_up, gla_b_gate, gla_norm, gla_w_out):
    batch, seq, d = x_prompt.shape
    dec_batch, dec_seq, _ = x_sample.shape
    depth = ada_w.shape[0]
    heads, head_k, head_v = state_gla.shape[3:]
    dk_tot = heads * head_k
    dv_tot = heads * head_v
    rank = gla_w_gate_up.shape[2]
    d_ff = ffn_w_down.shape[2]
    ctx_rows = batch * seq
    assert ctx_rows % dec_seq == 0 and dec_seq % seq == 0 and 1 + dec_batch <= SUBLANES
    n_ctx_groups = ctx_rows // dec_seq
    grp = dict(group_rows=dec_seq, n_ctx_groups=n_ctx_groups)
    tm_small = min(dec_seq, 512)
    tm_big = min(dec_seq, 1024)
    tf = 512 if d_ff % 512 == 0 else d_ff

    cond8 = jnp.zeros((SUBLANES, d), F32).at[0].set(c_ctx).at[1:1 + dec_batch].set(c)
    mods = _modulation(cond8, ada_w, ada_b)[:, :1 + dec_batch].reshape(depth, 1 + dec_batch, N_MOD, d)

    n_main = 4 * dk_tot + 2 * dv_tot
    w_in_t = jnp.swapaxes(gla_w_in, 1, 2)

    xs = (x_prompt.reshape(ctx_rows, d), x_sample.reshape(dec_batch * dec_seq, d))
    states = []
    for l in range(depth):
        mod = mods[l]
        j = l // 2
        npre = lambda s: norm_pre[l, s][None, :]
        npost = lambda s: norm_post[l, s][None, :]
        ffn = functools.partial(_ffn, mod=mod, wgu=ffn_w_gate_up, wd=ffn_w_down, layer=l, tf=tf, tm=tm_big,
                                **grp)
        (x,) = ffn(xs, gpre=npre(0), gpost=npost(0), slot=0, k=0, split_out=False)
        if l % 2 == 0:
            x_ctx = _fourier(x, mod, npre(1), t=seq, nb=batch, row_block0=0, mod0=lambda b: 0, k=1,
                             mix=(fourier_w, j, npost(1)), seqs=2 if batch % 2 == 0 else 1)
            f_lat = _fourier(x, mod, npre(1), t=dec_seq, nb=dec_batch, row_block0=n_ctx_groups,
                             mod0=lambda b: 1 + b, k=1)
            x_lat = _dense_post((f_lat,), fourier_w, x, mod, npost(1), widx=j, k=1, tm=tm_small,
                                tile0=ctx_rows // tm_small, **grp)
            x = (x_ctx, x_lat)
        else:
            zero = jnp.zeros((rank, dk_tot), F32)
            wz = jnp.concatenate([jnp.concatenate([gla_w_gate_up[j, 0], zero], axis=1),
                                  jnp.concatenate([zero, gla_w_gate_up[j, 1]], axis=1)], axis=0).astype(BF16)
            bz = gla_b_gate[j].reshape(1, 2 * dk_tot)
            colscale = jnp.ones((n_main,), F32).at[:dk_tot].set(head_k ** -0.5)
            colscale = colscale.at[2 * dk_tot:3 * dk_tot].set(head_k ** -0.5).reshape(1, n_main)
            p, g = _gla_in(x, mod, npre(1), w_in_t, wz, bz, colscale, widx=j, n_main=n_main, k=1,
                           tm=tm_big, n_steps=8, head_k=head_k, dk_tot=dk_tot, **grp)
            gn = gla_norm[j][None, :]
            scan = functools.partial(_gla_scan, p, g, gn, heads=heads, head_k=head_k, head_v=head_v)
            o_ctx, st = scan(None, t=seq, nb=batch, row_block0=0, hps=heads, want_state=True)
            (o_lat,) = scan(state_gla[:, j:j + 1], t=dec_seq, nb=dec_batch, row_block0=n_ctx_groups,
                            hps=min(heads, max(1, 2 * heads * seq // dec_seq)), want_state=False)
            states.append(st)
            x = _dense_post((o_ctx, o_lat), gla_w_out, x, mod, npost(1), widx=j, k=1, tm=tm_small, **grp)
        xs = tuple(ffn(x if isinstance(x, tuple) else (x,), gpre=npre(2), gpost=npost(2), slot=1, k=2,
                       split_out=l == depth - 1))

    y_prompt = xs[0].reshape(batch, seq, d)
    y_sample = xs[1].reshape(dec_batch, dec_seq, d)
    new_state = jnp.concatenate(states, axis=1)
    return (y_prompt, y_sample, new_state)
```
